```python
import jax, jax.numpy as jnp
from jax import lax
import numpy as np

D_MODEL = 4096
BATCH = 1
SEQ = 16384
DEPTH = 1
DEC_BATCH = 2
DEC_SEQ = 8192
PAST_LEN = 128

GRID_W = 64
N_HEADS = 16
HEAD_DIM = 128
D_ATTN = N_HEADS * HEAD_DIM
WIN_R = 8
WIN_C = 16
D_LRU = 2048
LRU_BLOCKS = 16
LRU_BLOCK = D_LRU // LRU_BLOCKS
LRU_CONV = 4
LRU_C = 8.0
D_FF = 3 * D_MODEL
FFN_CONV = 3
EPS = 1e-6
D_IN = 3 * D_ATTN + 2 * D_LRU + 2 * D_MODEL
SPLITS = (D_ATTN, 2 * D_ATTN, 3 * D_ATTN, 3 * D_ATTN + D_LRU,
          3 * D_ATTN + 2 * D_LRU, 3 * D_ATTN + 2 * D_LRU + D_MODEL)

kernel_name = "hybrid_bidir_natten_rglru_encoder"


def _rmsnorm(x, g):
    xf = x.astype(jnp.float32)
    var = jnp.mean(xf * xf, axis=-1, keepdims=True)
    return (xf * lax.rsqrt(var + EPS) * g.astype(jnp.float32)).astype(x.dtype)


def _dwconv(x, w, b, pad_lo, pad_hi):
    c = x.shape[-1]
    y = lax.conv_general_dilated(
        x, w[:, None, :].astype(x.dtype), window_strides=(1,),
        padding=[(pad_lo, pad_hi)], dimension_numbers=("NWC", "WIO", "NWC"),
        feature_group_count=c)
    return y + b.astype(x.dtype)


def _neighbourhood_attention(q, k, v, rpb):
    bsz, s = q.shape[0], q.shape[1]
    rows = s // GRID_W
    kr = min(WIN_R, rows)
    scale = HEAD_DIM ** -0.5
    qg = q.reshape(bsz, rows, GRID_W, N_HEADS, HEAD_DIM)
    kg = k.reshape(bsz, rows, GRID_W, N_HEADS, HEAD_DIM)
    vg = v.reshape(bsz, rows, GRID_W, N_HEADS, HEAD_DIM)
    cols = np.arange(GRID_W)
    col_start = np.clip(cols - WIN_C // 2, 0, GRID_W - WIN_C)
    col_idx = col_start[:, None] + np.arange(WIN_C)[None, :]
    dc = col_idx - cols[:, None] + (WIN_C - 1)
    bias_c = rpb.astype(jnp.float32)[:, :, dc]

    def row_fn(r):
        rs = jnp.clip(r - kr // 2, 0, rows - kr)
        q_r = lax.dynamic_index_in_dim(qg, r, axis=1, keepdims=False)
        k_rows = lax.dynamic_slice_in_dim(kg, rs, kr, axis=1)
        v_rows = lax.dynamic_slice_in_dim(vg, rs, kr, axis=1)
        k_win = k_rows[:, :, col_idx]
        v_win = v_rows[:, :, col_idx]
        sc = jnp.einsum("bqhd,biqjhd->bhqij", q_r, k_win,
                        preferred_element_type=jnp.float32) * scale
        dr = rs + jnp.arange(kr) - r + (WIN_R - 1)
        bias = jnp.transpose(bias_c[:, dr], (0, 2, 1, 3))
        sc = sc + bias[None]
        p = jax.nn.softmax(sc.reshape(bsz, N_HEADS, GRID_W, kr * WIN_C), axis=-1)
        p = p.reshape(bsz, N_HEADS, GRID_W, kr, WIN_C).astype(v.dtype)
        return jnp.einsum("bhqij,biqjhd->bqhd", p, v_win)

    out = lax.map(row_fn, jnp.arange(rows))
    return jnp.transpose(out, (1, 0, 2, 3, 4)).reshape(bsz, s, D_ATTN)


def _lin_combine(e1, e2):
    a1, b1 = e1
    a2, b2 = e2
    return a1 * a2, a2 * b1 + b2


def _rglru_bidir(x, wa, ba, wx, bx, lam):
    bsz, s, _ = x.shape
    xf = x.astype(jnp.float32)
    xb = xf.reshape(bsz, s, LRU_BLOCKS, LRU_BLOCK)

    def one_dir(d, reverse):
        r = jax.nn.sigmoid(jnp.einsum("bsnk,nkj->bsnj", xb, wa[d].astype(jnp.float32)).reshape(bsz, s, D_LRU)
                           + ba[d].astype(jnp.float32))
        i = jax.nn.sigmoid(jnp.einsum("bsnk,nkj->bsnj", xb, wx[d].astype(jnp.float32)).reshape(bsz, s, D_LRU)
                           + bx[d].astype(jnp.float32))
        log_a = -LRU_C * r * jax.nn.softplus(-lam[d].astype(jnp.float32))
        a = jnp.exp(log_a)
        mult = jnp.sqrt(-jnp.expm1(2.0 * log_a))
        _, h = lax.associative_scan(_lin_combine, (a, mult * i * xf), axis=1, reverse=reverse)
        return h

    return (one_dir(0, False) + one_dir(1, True)).astype(x.dtype)


def _encoder_layer(x, ln_mix_g, w_in, b_gate, rpb, lru_conv_w, lru_conv_b, lru_wa, lru_ba,
                   lru_wx, lru_bx, lru_lambda, w_o_attn, w_o_lru, w_out, ln_ffn_g,
                   w_ffn_in, ffn_conv_w, ffn_conv_b, w_ffn_out):
    bsz, s, _ = x.shape
    xn = _rmsnorm(x, ln_mix_g)
    z = xn @ w_in.astype(x.dtype)
    q, k, v, xl, yl, ga, gr = jnp.split(z, SPLITS, axis=-1)
    heads = lambda t: t.reshape(bsz, s, N_HEADS, HEAD_DIM)
    attn = _neighbourhood_attention(heads(q), heads(k), heads(v), rpb)
    xl = _dwconv(xl, lru_conv_w, lru_conv_b, LRU_CONV // 2, LRU_CONV - 1 - LRU_CONV // 2)
    lru = _rglru_bidir(xl, lru_wa, lru_ba, lru_wx, lru_bx, lru_lambda) * jax.nn.gelu(yl)
    merged = (jax.nn.sigmoid(ga + b_gate[0].astype(x.dtype)) * (attn @ w_o_attn.astype(x.dtype))
              + jax.nn.sigmoid(gr + b_gate[1].astype(x.dtype)) * (lru @ w_o_lru.astype(x.dtype)))
    x = x + merged @ w_out.astype(x.dtype)
    hn = _rmsnorm(x, ln_ffn_g)
    gate, val = jnp.split(hn @ w_ffn_in.astype(x.dtype), 2, axis=-1)
    gate = _dwconv(gate, ffn_conv_w, ffn_conv_b, FFN_CONV // 2, FFN_CONV // 2)
    return x + (jax.nn.gelu(gate) * val) @ w_ffn_out.astype(x.dtype)


def setup_inputs(seed: int = 0) -> dict:
    key = jax.random.key(seed)
    ks = jax.random.split(key, 24)
    f32 = jnp.float32
    nrm = lambda k, shape, sc: jax.random.normal(k, shape, f32) * sc
    u = jax.random.uniform(ks[11], (DEPTH, 2, D_LRU), f32, 0.9, 0.999)
    a0 = u ** (1.0 / LRU_C)
    lru_lambda = jnp.log(a0) - jnp.log1p(-a0)
    return {
        "x_prompt": nrm(ks[0], (BATCH, SEQ, D_MODEL), 1.0),
        "x_sample": nrm(ks[1], (DEC_BATCH, DEC_SEQ, D_MODEL), 1.0),
        "ln_mix_g": 1.0 + nrm(ks[2], (DEPTH, D_MODEL), 0.05),
        "w_in": nrm(ks[3], (DEPTH, D_MODEL, D_IN), D_MODEL ** -0.5),
        "b_gate": nrm(ks[4], (DEPTH, 2, D_MODEL), 0.1),
        "rpb": nrm(ks[5], (DEPTH, N_HEADS, 2 * WIN_R - 1, 2 * WIN_C - 1), 0.2),
        "lru_conv_w": nrm(ks[6], (DEPTH, LRU_CONV, D_LRU), LRU_CONV ** -0.5),
        "lru_conv_b": nrm(ks[7], (DEPTH, D_LRU), 0.02),
        "lru_wa": nrm(ks[8], (DEPTH, 2, LRU_BLOCKS, LRU_BLOCK, LRU_BLOCK), LRU_BLOCK ** -0.5),
        "lru_ba": nrm(ks[9], (DEPTH, 2, D_LRU), 0.1),
        "lru_wx": nrm(ks[10], (DEPTH, 2, LRU_BLOCKS, LRU_BLOCK, LRU_BLOCK), LRU_BLOCK ** -0.5),
        "lru_bx": nrm(ks[12], (DEPTH, 2, D_LRU), 0.1),
        "lru_lambda": lru_lambda,
        "w_o_attn": nrm(ks[13], (DEPTH, D_ATTN, D_MODEL), D_ATTN ** -0.5),
        "w_o_lru": nrm(ks[14], (DEPTH, D_LRU, D_MODEL), D_LRU ** -0.5),
        "w_out": nrm(ks[15], (DEPTH, D_MODEL, D_MODEL), D_MODEL ** -0.5),
        "ln_ffn_g": 1.0 + nrm(ks[16], (DEPTH, D_MODEL), 0.05),
        "w_ffn_in": nrm(ks[17], (DEPTH, D_MODEL, 2 * D_FF), D_MODEL ** -0.5),
        "ffn_conv_w": nrm(ks[18], (DEPTH, FFN_CONV, D_FF), FFN_CONV ** -0.5),
        "ffn_conv_b": nrm(ks[19], (DEPTH, D_FF), 0.02),
        "w_ffn_out": nrm(ks[20], (DEPTH, D_FF, D_MODEL), D_FF ** -0.5),
        "ln_final_g": 1.0 + nrm(ks[21], (D_MODEL,), 0.05),
    }


def _trunk(x, ln_mix_g, w_in, b_gate, rpb, lru_conv_w, lru_conv_b, lru_wa, lru_ba, lru_wx,
           lru_bx, lru_lambda, w_o_attn, w_o_lru, w_out, ln_ffn_g, w_ffn_in, ffn_conv_w,
           ffn_conv_b, w_ffn_out, ln_final_g):
    for l in range(DEPTH):
        x = _encoder_layer(x, ln_mix_g[l], w_in[l], b_gate[l], rpb[l], lru_conv_w[l], lru_conv_b[l],
                           lru_wa[l], lru_ba[l], lru_wx[l], lru_bx[l], lru_lambda[l], w_o_attn[l],
                           w_o_lru[l], w_out[l], ln_ffn_g[l], w_ffn_in[l], ffn_conv_w[l],
                           ffn_conv_b[l], w_ffn_out[l])
    return _rmsnorm(x, ln_final_g)


def reference(x_prompt, x_sample, ln_mix_g, w_in, b_gate, rpb, lru_conv_w, lru_conv_b, lru_wa,
              lru_ba, lru_wx, lru_bx, lru_lambda, w_o_attn, w_o_lru, w_out, ln_ffn_g, w_ffn_in,
              ffn_conv_w, ffn_conv_b, w_ffn_out, ln_final_g):
    y_prompt = _trunk(x_prompt, ln_mix_g, w_in, b_gate, rpb, lru_conv_w, lru_conv_b, lru_wa, lru_ba,
                      lru_wx, lru_bx, lru_lambda, w_o_attn, w_o_lru, w_out, ln_ffn_g, w_ffn_in,
                      ffn_conv_w, ffn_conv_b, w_ffn_out, ln_final_g)
    y_sample = _trunk(x_sample, ln_mix_g, w_in, b_gate, rpb, lru_conv_w, lru_conv_b, lru_wa, lru_ba,
                      lru_wx, lru_bx, lru_lambda, w_o_attn, w_o_lru, w_out, ln_ffn_g, w_ffn_in,
                      ffn_conv_w, ffn_conv_b, w_ffn_out, ln_final_g)
    return (y_prompt, y_sample)
```

```python
import functools

import jax
import jax.numpy as jnp
import numpy as np
from jax import lax
from jax.experimental import pallas as pl
from jax.experimental.pallas import tpu as pltpu

EPS = 1e-6
GRID_W = 64
N_HEADS = 16
HEAD_DIM = 128
WIN_R = 8
WIN_C = 16
LRU_BLOCKS = 16
LRU_BLOCK = 128
LRU_CONV = 4
LRU_C = 8.0
FFN_CONV = 3
FFN_OUT_COLS = 1024
MASK_VALUE = -1e30

SUBLANES = 8
VMEM_LIMIT_BYTES = 56 * 1024 * 1024

F32 = jnp.float32
BF16 = jnp.bfloat16


def _params(*semantics):
    return pltpu.CompilerParams(dimension_semantics=semantics, vmem_limit_bytes=VMEM_LIMIT_BYTES)


def _gelu_tanh(x):
    return 0.5 * x * (1.0 + jnp.tanh(np.sqrt(2.0 / np.pi).astype(np.float32) * (x + 0.044715 * (x * x * x))))


def _rmsnorm_body(x_ref, g_ref, o_ref):
    x = x_ref[...]
    var = jnp.mean(x * x, axis=-1, keepdims=True)
    o_ref[...] = (x * lax.rsqrt(var + EPS) * g_ref[...]).astype(o_ref.dtype)


def _rmsnorm(x, g, out_dtype, tm):
    t, d = x.shape
    return pl.pallas_call(
        _rmsnorm_body,
        grid=(t // tm,),
        in_specs=[pl.BlockSpec((tm, d), lambda i: (i, 0)), pl.BlockSpec((1, d), lambda i: (0, 0))],
        out_specs=pl.BlockSpec((tm, d), lambda i: (i, 0)),
        out_shape=jax.ShapeDtypeStruct((t, d), out_dtype),
        compiler_params=_params("parallel"),
        name="rmsnorm",
    )(x, g.reshape(1, d))


def _matmul_body(x_ref, w_ref, o_ref):
    o_ref[...] = jnp.dot(x_ref[...], w_ref[...], preferred_element_type=F32).astype(o_ref.dtype)


def _matmul_res_body(x_ref, w_ref, r_ref, o_ref):
    o_ref[...] = r_ref[...] + jnp.dot(x_ref[...], w_ref[...], preferred_element_type=F32)


def _matmul(x, w, col0, ncols, out_dtype, tm, tn, residual=None, name="matmul"):
    t, k = x.shape
    assert w.shape[0] == k and col0 % tn == 0 and ncols % tn == 0 and t % tm == 0
    jb = col0 // tn
    in_specs = [pl.BlockSpec((tm, k), lambda i, j: (i, 0)), pl.BlockSpec((k, tn), lambda i, j: (0, j + jb))]
    args = [x, w]
    body = _matmul_body
    if residual is not None:
        in_specs.append(pl.BlockSpec((tm, tn), lambda i, j: (i, j)))
        args.append(residual)
        body = _matmul_res_body
    return pl.pallas_call(
        body,
        grid=(t // tm, ncols // tn),
        in_specs=in_specs,
        out_specs=pl.BlockSpec((tm, tn), lambda i, j: (i, j)),
        out_shape=jax.ShapeDtypeStruct((t, ncols), out_dtype),
        compiler_params=_params("parallel", "parallel"),
        name=name,
    )(*args)


def _attn_bias_table(rpb):
    cols = np.arange(GRID_W)
    col_start = np.clip(cols - WIN_C // 2, 0, GRID_W - WIN_C)
    key = np.arange(GRID_W)[None, :]
    valid = (key >= col_start[:, None]) & (key < col_start[:, None] + WIN_C)
    dc = np.clip(key - cols[:, None] + (WIN_C - 1), 0, 2 * WIN_C - 2)
    g = rpb.astype(F32)[:, :, dc]
    g = jnp.where(valid[None, None], g, MASK_VALUE)
    tbl = jnp.stack([g[:, v:v + WIN_R] for v in range(WIN_R)], axis=1)
    tbl = jnp.transpose(tbl, (0, 1, 3, 2, 4))
    return tbl.reshape(N_HEADS, WIN_R, GRID_W, WIN_R * GRID_W)


def _attn_body(q_ref, k_ref, v_ref, bias_ref, o_ref, *, rows, unroll):
    scale = HEAD_DIM ** -0.5
    nkeys = WIN_R * GRID_W

    def row_step(r, carry):
        rs = jnp.clip(r - WIN_R // 2, 0, rows - WIN_R)
        dr0 = rs - r + (WIN_R - 1)
        q0 = pl.multiple_of(r * GRID_W, GRID_W)
        k0 = pl.multiple_of(rs * GRID_W, GRID_W)
        q = q_ref[pl.ds(q0, GRID_W), :]
        kw = k_ref[pl.ds(k0, nkeys), :]
        vw = v_ref[pl.ds(k0, nkeys), :]
        s = lax.dot_general(q, kw, (((1,), (1,)), ((), ())), preferred_element_type=F32) * scale
        s = s + bias_ref[0, dr0]
        m = jnp.max(s, axis=-1, keepdims=True)
        p = jnp.exp(s - m)
        p = p / jnp.sum(p, axis=-1, keepdims=True)
        o = jnp.dot(p.astype(BF16), vw, preferred_element_type=F32)
        o_ref[pl.ds(q0, GRID_W), :] = o.astype(o_ref.dtype)
        return carry

    lax.fori_loop(0, rows, row_step, 0, unroll=unroll)


def _attention(qkv, bias_tbl, bsz, seq):
    rows = seq // GRID_W
    assert seq % GRID_W == 0 and rows >= WIN_R
    blk = lambda part: pl.BlockSpec((seq, HEAD_DIM), lambda b, h: (b, part * N_HEADS + h))
    return pl.pallas_call(
        functools.partial(_attn_body, rows=rows, unroll=2),
        grid=(bsz, N_HEADS),
        in_specs=[blk(0), blk(1), blk(2),
                  pl.BlockSpec((1, WIN_R, GRID_W, WIN_R * GRID_W), lambda b, h: (h, 0, 0, 0))],
        out_specs=pl.BlockSpec((seq, HEAD_DIM), lambda b, h: (b, h)),
        out_shape=jax.ShapeDtypeStruct((bsz * seq, N_HEADS * HEAD_DIM), BF16),
        compiler_params=_params("parallel", "parallel"),
        name="natten",
    )(qkv, qkv, qkv, bias_tbl)


def _lru_body(*refs, reverse, nchunks, tc):
    if reverse:
        (xp_ref, xc_ref, xn_ref, yl_ref, hf_ref, cw_ref, cb_ref, w_ref, bias_ref, lam_ref,
         o_ref, xpad, a_scr, b_scr, hs_scr, h_scr) = refs
    else:
        (xp_ref, xc_ref, xn_ref, cw_ref, cb_ref, w_ref, bias_ref, lam_ref,
         o_ref, xpad, a_scr, b_scr, h_scr) = refs
        hs_scr = o_ref
    c = pl.program_id(1)
    ci = nchunks - 1 - c if reverse else c

    @pl.when(c == 0)
    def _():
        h_scr[...] = jnp.zeros_like(h_scr)

    xpad[0:SUBLANES, :] = jnp.where(ci == 0, 0.0, xp_ref[...])
    xpad[SUBLANES:SUBLANES + tc, :] = xc_ref[...]
    xpad[SUBLANES + tc:2 * SUBLANES + tc, :] = jnp.where(ci == nchunks - 1, 0.0, xn_ref[...])

    lam = -lam_ref[...]
    softplus = jnp.maximum(lam, 0.0) + jnp.log1p(jnp.exp(-jnp.abs(lam)))

    for n in range(LRU_BLOCKS):
        lanes = slice(n * LRU_BLOCK, (n + 1) * LRU_BLOCK)
        xc = cb_ref[:, lanes]
        for k in range(LRU_CONV):
            xc = xc + cw_ref[k:k + 1, lanes] * xpad[pl.ds(SUBLANES - LRU_CONV // 2 + k, tc), lanes]
        u = jnp.dot(xc.astype(BF16), w_ref[n], preferred_element_type=F32)
        r = jax.nn.sigmoid(u[:, :LRU_BLOCK] + bias_ref[0:1, lanes])
        i = jax.nn.sigmoid(u[:, LRU_BLOCK:] + bias_ref[1:2, lanes])
        log_a = -LRU_C * r * softplus[:, lanes]
        a = jnp.exp(log_a)
        mult = jnp.sqrt(-jnp.tanh(log_a) * (a * a + 1.0))
        a_scr[:, lanes] = a
        b_scr[:, lanes] = mult * i * xc

    def group_step(g, h):
        base = pl.multiple_of(g * SUBLANES, SUBLANES)
        for j in range(SUBLANES):
            row = base + (SUBLANES - 1 - j if reverse else j)
            h = a_scr[pl.ds(row, 1), :] * h + b_scr[pl.ds(row, 1), :]
            hs_scr[pl.ds(row, 1), :] = h
        return h

    ngroups = tc // SUBLANES
    if reverse:
        h = lax.fori_loop(0, ngroups, lambda g, h: group_step(ngroups - 1 - g, h), h_scr[...])
    else:
        h = lax.fori_loop(0, ngroups, group_step, h_scr[...])
    h_scr[...] = h

    if reverse:
        o_ref[...] = ((hf_ref[...] + hs_scr[...]) * _gelu_tanh(yl_ref[...])).astype(o_ref.dtype)


def _lru_direction(xlyl, h_fwd, d, conv_w, conv_b, wa, ba, wx, bx, lam, bsz, seq, tc):
    t = bsz * seq
    dl = LRU_BLOCKS * LRU_BLOCK
    reverse = d == 1
    nchunks = seq // tc
    assert seq % tc == 0 and tc % SUBLANES == 0
    hb = tc // SUBLANES

    def chunk(b, c):
        return b * nchunks + (nchunks - 1 - c if reverse else c)

    cur = lambda col: pl.BlockSpec((tc, dl), lambda b, c: (chunk(b, c), col))
    const = lambda shape: pl.BlockSpec(shape, lambda b, c: (0,) * len(shape))
    in_specs = [
        pl.BlockSpec((SUBLANES, dl), lambda b, c: (jnp.maximum(chunk(b, c) * hb - 1, 0), 0)),
        cur(0),
        pl.BlockSpec((SUBLANES, dl), lambda b, c: (jnp.minimum((chunk(b, c) + 1) * hb, t // SUBLANES - 1), 0)),
    ]
    args = [xlyl, xlyl, xlyl]
    if reverse:
        in_specs += [cur(1), cur(0)]
        args += [xlyl, h_fwd]
    w = jnp.concatenate([wa[d], wx[d]], axis=-1).astype(BF16)
    bias = jnp.stack([ba[d], bx[d]]).astype(F32)
    in_specs += [const((LRU_CONV, dl)), const((1, dl)), const((LRU_BLOCKS, LRU_BLOCK, 2 * LRU_BLOCK)),
                 const((2, dl)), const((1, dl))]
    args += [conv_w.astype(F32), conv_b.reshape(1, dl).astype(F32), w, bias, lam[d].reshape(1, dl).astype(F32)]
    scratch = [pltpu.VMEM((tc + 2 * SUBLANES, dl), F32), pltpu.VMEM((tc, dl), F32), pltpu.VMEM((tc, dl), F32)]
    if reverse:
        scratch.append(pltpu.VMEM((tc, dl), F32))
    scratch.append(pltpu.VMEM((1, dl), F32))
    return pl.pallas_call(
        functools.partial(_lru_body, reverse=reverse, nchunks=nchunks, tc=tc),
        grid=(bsz, nchunks),
        in_specs=in_specs,
        out_specs=pl.BlockSpec((tc, dl), lambda b, c: (chunk(b, c), 0)),
        out_shape=jax.ShapeDtypeStruct((t, dl), BF16 if reverse else F32),
        scratch_shapes=scratch,
        compiler_params=_params("parallel", "arbitrary"),
        name="rglru_bwd" if reverse else "rglru_fwd",
    )(*args)


def _merge_body(a_ref, r_ref, wa_ref, wr_ref, ga_ref, gr_ref, bg_ref, o_ref):
    pa = jnp.dot(a_ref[...], wa_ref[...], preferred_element_type=F32)
    pr = jnp.dot(r_ref[...], wr_ref[...], preferred_element_type=F32)
    o = jax.nn.sigmoid(ga_ref[...] + bg_ref[0:1, :]) * pa + jax.nn.sigmoid(gr_ref[...] + bg_ref[1:2, :]) * pr
    o_ref[...] = o.astype(o_ref.dtype)


def _merge(attn, lru, w_oa, w_or, gates, b_gate, tm, tn):
    t, da = attn.shape
    d = w_oa.shape[1]
    nj = d // tn
    return pl.pallas_call(
        _merge_body,
        grid=(t // tm, nj),
        in_specs=[pl.BlockSpec((tm, da), lambda i, j: (i, 0)),
                  pl.BlockSpec((tm, lru.shape[1]), lambda i, j: (i, 0)),
                  pl.BlockSpec((da, tn), lambda i, j: (0, j)),
                  pl.BlockSpec((lru.shape[1], tn), lambda i, j: (0, j)),
                  pl.BlockSpec((tm, tn), lambda i, j: (i, j)),
                  pl.BlockSpec((tm, tn), lambda i, j: (i, j + nj)),
                  pl.BlockSpec((2, tn), lambda i, j: (0, j))],
        out_specs=pl.BlockSpec((tm, tn), lambda i, j: (i, j)),
        out_shape=jax.ShapeDtypeStruct((t, d), BF16),
        compiler_params=_params("parallel", "parallel"),
        name="gated_merge",
    )(attn, lru, w_oa, w_or, gates, gates, b_gate.astype(F32))


def _ffn_out_body(gp_ref, gc_ref, gn_ref, val_ref, cw_ref, cb_ref, w_ref, x_ref, lng_ref, o_ref, gpad,
                  *, tiles_per_seq, tm):
    i = pl.program_id(0)
    k = pl.program_id(1)
    si = i % tiles_per_seq
    gpad[0:SUBLANES, :] = jnp.where(si == 0, 0.0, gp_ref[...])
    gpad[SUBLANES:SUBLANES + tm, :] = gc_ref[...]
    gpad[SUBLANES + tm:2 * SUBLANES + tm, :] = jnp.where(si == tiles_per_seq - 1, 0.0, gn_ref[...])
    conv = cb_ref[...]
    for j in range(FFN_CONV):
        conv = conv + cw_ref[j:j + 1, :] * gpad[pl.ds(SUBLANES - FFN_CONV // 2 + j, tm), :]
    u = (_gelu_tanh(conv) * val_ref[...]).astype(BF16)

    @pl.when(k == 0)
    def _():
        o_ref[...] = x_ref[...]

    d = o_ref.shape[1]
    for n0 in range(0, d, FFN_OUT_COLS):
        cols = slice(n0, min(n0 + FFN_OUT_COLS, d))
        o_ref[:, cols] += jnp.dot(u, w_ref[:, cols], preferred_element_type=F32)

    @pl.when(k == pl.num_programs(1) - 1)
    def _():
        x = o_ref[...]
        var = jnp.mean(x * x, axis=-1, keepdims=True)
        o_ref[...] = x * lax.rsqrt(var + EPS) * lng_ref[...]


def _ffn_out(gv, conv_w, conv_b, w_down, x1, ln_g, seq, tm, tk):
    t = gv.shape[0]
    dff, d = w_down.shape
    nk = dff // tk
    hb = tm // SUBLANES
    assert seq % tm == 0 and dff % tk == 0
    return pl.pallas_call(
        functools.partial(_ffn_out_body, tiles_per_seq=seq // tm, tm=tm),
        grid=(t // tm, nk),
        in_specs=[pl.BlockSpec((SUBLANES, tk), lambda i, k: (jnp.maximum(i * hb - 1, 0), k)),
                  pl.BlockSpec((tm, tk), lambda i, k: (i, k)),
                  pl.BlockSpec((SUBLANES, tk), lambda i, k: (jnp.minimum((i + 1) * hb, t // SUBLANES - 1), k)),
                  pl.BlockSpec((tm, tk), lambda i, k: (i, k + nk)),
                  pl.BlockSpec((FFN_CONV, tk), lambda i, k: (0, k)),
                  pl.BlockSpec((1, tk), lambda i, k: (0, k)),
                  pl.BlockSpec((tk, d), lambda i, k: (k, 0)),
                  pl.BlockSpec((tm, d), lambda i, k: (i, 0), pipeline_mode=pl.Buffered(1)),
                  pl.BlockSpec((1, d), lambda i, k: (0, 0))],
        out_specs=pl.BlockSpec((tm, d), lambda i, k: (i, 0)),
        out_shape=jax.ShapeDtypeStruct((t, d), F32),
        scratch_shapes=[pltpu.VMEM((tm + 2 * SUBLANES, tk), F32)],
        compiler_params=_params("parallel", "arbitrary"),
        name="ffn_out",
    )(gv, gv, gv, gv, conv_w.astype(F32), conv_b.reshape(1, dff).astype(F32), w_down, x1,
      ln_g.reshape(1, d).astype(F32))


def _tile(n, pref):
    return pref if n % pref == 0 else n


def _trunk(x, w, bias_tbl):
    bsz, seq, d = x.shape
    t = bsz * seq
    x = x.reshape(t, d)
    da = N_HEADS * HEAD_DIM
    dl = LRU_BLOCKS * LRU_BLOCK
    dff = w["w_ffn_out"].shape[0]
    tm = _tile(t, 1024)

    xn = _rmsnorm(x, w["ln_mix_g"], BF16, _tile(t, 256))
    qkv = _matmul(xn, w["w_in"], 0, 3 * da, BF16, tm, 1024, name="in_proj_qkv")
    xlyl = _matmul(xn, w["w_in"], 3 * da, 2 * dl, F32, tm, 1024, name="in_proj_lru")
    gates = _matmul(xn, w["w_in"], 3 * da + 2 * dl, 2 * d, F32, tm, 1024, name="in_proj_gates")

    attn = _attention(qkv, bias_tbl, bsz, seq)

    lru_args = (w["lru_conv_w"], w["lru_conv_b"], w["lru_wa"], w["lru_ba"], w["lru_wx"], w["lru_bx"],
                w["lru_lambda"], bsz, seq, _tile(seq, 256))
    h_fwd = _lru_direction(xlyl, None, 0, *lru_args)
    lru = _lru_direction(xlyl, h_fwd, 1, *lru_args)

    merged = _merge(attn, lru, w["w_o_attn"], w["w_o_lru"], gates, w["b_gate"], tm, 512)
    x1 = _matmul(merged, w["w_out"], 0, d, F32, tm, 512, residual=x, name="out_proj")

    hn = _rmsnorm(x1, w["ln_ffn_g"], BF16, _tile(t, 256))
    gv = _matmul(hn, w["w_ffn_in"], 0, 2 * dff, F32, tm, 1024, name="ffn_in")
    y = _ffn_out(gv, w["ffn_conv_w"], w["ffn_conv_b"], w["w_ffn_out"], x1, w["ln_final_g"], seq,
                 _tile(seq, 512), 512)
    return y.reshape(bsz, seq, d)


def kernel(x_prompt, x_sample, ln_mix_g, w_in, b_gate, rpb, lru_conv_w, lru_conv_b, lru_wa, lru_ba, lru_wx,
           lru_bx, lru_lambda, w_o_attn, w_o_lru, w_out, ln_ffn_g, w_ffn_in, ffn_conv_w, ffn_conv_b, w_ffn_out,
           ln_final_g):
    assert ln_mix_g.shape[0] == 1, "single-layer block"
    w = dict(
        ln_mix_g=ln_mix_g[0].astype(F32), w_in=w_in[0].astype(BF16), b_gate=b_gate[0],
        lru_conv_w=lru_conv_w[0], lru_conv_b=lru_conv_b[0], lru_wa=lru_wa[0], lru_ba=lru_ba[0],
        lru_wx=lru_wx[0], lru_bx=lru_bx[0], lru_lambda=lru_lambda[0],
        w_o_attn=w_o_attn[0].astype(BF16), w_o_lru=w_o_lru[0].astype(BF16), w_out=w_out[0].astype(BF16),
        ln_ffn_g=ln_ffn_g[0].astype(F32), w_ffn_in=w_ffn_in[0].astype(BF16),
        ffn_conv_w=ffn_conv_w[0], ffn_conv_b=ffn_conv_b[0], w_ffn_out=w_ffn_out[0].astype(BF16),
        ln_final_g=ln_final_g.astype(F32),
    )
    bias_tbl = _attn_bias_table(rpb[0])
    return (_trunk(x_prompt, w, bias_tbl), _trunk(x_sample, w, bias_tbl))
```

```python
import functools

import jax
import jax.numpy as jnp
import numpy as np
from jax import lax
from jax.experimental import pallas as pl
from jax.experimental.pallas import tpu as pltpu

EPS = 1e-6
GRID_W = 64
N_HEADS = 16
HEAD_DIM = 128
WIN_R = 8
WIN_C = 16
ATTN_GROUP = 8
LRU_BLOCKS = 16
LRU_BLOCK = 128
LRU_CONV = 4
LRU_C = 8.0
FFN_CONV = 3
FFN_OUT_COLS = 1024
MASK_VALUE = -1e30

SUBLANES = 8
BF16_ROWS = 16
EW_ROWS = 16
VMEM_LIMIT_BYTES = 56 * 1024 * 1024

F32 = jnp.float32
BF16 = jnp.bfloat16


def _params(*semantics):
    return pltpu.CompilerParams(dimension_semantics=semantics, vmem_limit_bytes=VMEM_LIMIT_BYTES)


def _gelu_tanh(x):
    return 0.5 * x * (1.0 + jnp.tanh(np.sqrt(2.0 / np.pi).astype(np.float32) * (x + 0.044715 * (x * x * x))))


def _rmsnorm_body(x_ref, g_ref, o_ref):
    x = x_ref[...]
    var = jnp.mean(x * x, axis=-1, keepdims=True)
    o_ref[...] = (x * lax.rsqrt(var + EPS) * g_ref[...]).astype(o_ref.dtype)


def _rmsnorm(x, g, out_dtype, tm):
    t, d = x.shape
    return pl.pallas_call(
        _rmsnorm_body,
        grid=(t // tm,),
        in_specs=[pl.BlockSpec((tm, d), lambda i: (i, 0)), pl.BlockSpec((1, d), lambda i: (0, 0))],
        out_specs=pl.BlockSpec((tm, d), lambda i: (i, 0)),
        out_shape=jax.ShapeDtypeStruct((t, d), out_dtype),
        compiler_params=_params("parallel"),
        name="rmsnorm",
    )(x, g.reshape(1, d))


def _matmul_body(x_ref, w_ref, o_ref):
    o_ref[...] = jnp.dot(x_ref[...], w_ref[...], preferred_element_type=F32).astype(o_ref.dtype)


def _matmul_res_body(x_ref, w_ref, r_ref, o_ref):
    o_ref[...] = r_ref[...] + jnp.dot(x_ref[...], w_ref[...], preferred_element_type=F32)


def _matmul(x, w, col0, ncols, out_dtype, tm, tn, residual=None, name="matmul"):
    t, k = x.shape
    assert w.shape[0] == k and col0 % tn == 0 and ncols % tn == 0 and t % tm == 0
    jb = col0 // tn
    in_specs = [pl.BlockSpec((tm, k), lambda i, j: (i, 0)), pl.BlockSpec((k, tn), lambda i, j: (0, j + jb))]
    args = [x, w]
    body = _matmul_body
    if residual is not None:
        in_specs.append(pl.BlockSpec((tm, tn), lambda i, j: (i, j)))
        args.append(residual)
        body = _matmul_res_body
    return pl.pallas_call(
        body,
        grid=(t // tm, ncols // tn),
        in_specs=in_specs,
        out_specs=pl.BlockSpec((tm, tn), lambda i, j: (i, j)),
        out_shape=jax.ShapeDtypeStruct((t, ncols), out_dtype),
        compiler_params=_params("parallel", "parallel"),
        name=name,
    )(*args)


def _attn_bias_table(rpb):
    cols = np.arange(GRID_W)
    col_start = np.clip(cols - WIN_C // 2, 0, GRID_W - WIN_C)
    key = np.arange(GRID_W)[None, :]
    valid = (key >= col_start[:, None]) & (key < col_start[:, None] + WIN_C)
    dc = np.clip(key - cols[:, None] + (WIN_C - 1), 0, 2 * WIN_C - 2)
    g = rpb.astype(F32)[:, :, dc]
    g = jnp.where(valid[None, None], g, MASK_VALUE)
    tbl = jnp.stack([g[:, v:v + WIN_R] for v in range(WIN_R)], axis=1)
    tbl = jnp.transpose(tbl, (0, 1, 3, 2, 4))
    return tbl.reshape(N_HEADS, WIN_R, GRID_W, WIN_R * GRID_W)


def _attn_body(q_ref, k_ref, v_ref, bias_ref, o_ref, s_scr, p_scr, *, rows):
    scale = HEAD_DIM ** -0.5
    nkeys = WIN_R * GRID_W
    ngroups = rows // ATTN_GROUP

    def window(g, j):
        r = g * ATTN_GROUP + j
        rs = jnp.clip(r - WIN_R // 2, 0, rows - WIN_R)
        return pl.multiple_of(r * GRID_W, GRID_W), pl.multiple_of(rs * GRID_W, GRID_W), rs - r + (WIN_R - 1)

    def scores(g):
        for j in range(ATTN_GROUP):
            q0, k0, _ = window(g, j)
            s_scr[j] = lax.dot_general(q_ref[pl.ds(q0, GRID_W), :], k_ref[pl.ds(k0, nkeys), :],
                                       (((1,), (1,)), ((), ())), preferred_element_type=F32)

    def softmax(g):
        for j in range(ATTN_GROUP):
            _, _, dr0 = window(g, j)
            s = s_scr[j] * scale + bias_ref[0, dr0]
            p = jnp.exp(s - jnp.max(s, axis=-1, keepdims=True))
            p = p / jnp.sum(p, axis=-1, keepdims=True)
            p_scr[j] = p.astype(BF16)

    def values(g):
        for j in range(ATTN_GROUP):
            q0, k0, _ = window(g, j)
            o = jnp.dot(p_scr[j], v_ref[pl.ds(k0, nkeys), :], preferred_element_type=F32)
            o_ref[pl.ds(q0, GRID_W), :] = o.astype(o_ref.dtype)

    scores(0)
    softmax(0)
    scores(1)

    def steady(g, carry):
        values(g - 2)
        softmax(g - 1)
        scores(g)
        return carry

    lax.fori_loop(2, ngroups, steady, 0)
    values(ngroups - 2)
    softmax(ngroups - 1)
    values(ngroups - 1)


def _attention(qkv, bias_tbl, bsz, seq):
    rows = seq // GRID_W
    nkeys = WIN_R * GRID_W
    assert seq % GRID_W == 0 and rows % ATTN_GROUP == 0 and rows // ATTN_GROUP >= 2 and rows >= WIN_R
    blk = lambda part: pl.BlockSpec((seq, HEAD_DIM), lambda b, h: (b, part * N_HEADS + h))
    return pl.pallas_call(
        functools.partial(_attn_body, rows=rows),
        grid=(bsz, N_HEADS),
        in_specs=[blk(0), blk(1), blk(2),
                  pl.BlockSpec((1, WIN_R, GRID_W, nkeys), lambda b, h: (h, 0, 0, 0))],
        out_specs=pl.BlockSpec((seq, HEAD_DIM), lambda b, h: (b, h)),
        out_shape=jax.ShapeDtypeStruct((bsz * seq, N_HEADS * HEAD_DIM), BF16),
        scratch_shapes=[pltpu.VMEM((ATTN_GROUP, GRID_W, nkeys), F32), pltpu.VMEM((ATTN_GROUP, GRID_W, nkeys), BF16)],
        compiler_params=_params("parallel", "parallel"),
        name="natten",
    )(qkv, qkv, qkv, bias_tbl)


def _lru_body(*refs, reverse, nchunks, tc):
    if reverse:
        (xp_ref, xc_ref, xn_ref, yl_ref, hf_ref, cw_ref, cb_ref, w_ref, bias_ref, lam_ref,
         o_ref, xpad, a_scr, b_scr, hs_scr, h_scr) = refs
    else:
        (xp_ref, xc_ref, xn_ref, cw_ref, cb_ref, w_ref, bias_ref, lam_ref,
         o_ref, xpad, a_scr, b_scr, h_scr) = refs
        hs_scr = o_ref
    c = pl.program_id(1)
    ci = nchunks - 1 - c if reverse else c

    @pl.when(c == 0)
    def _():
        h_scr[...] = jnp.zeros_like(h_scr)

    xpad[0:SUBLANES, :] = jnp.where(ci == 0, 0.0, xp_ref[...])
    xpad[SUBLANES:SUBLANES + tc, :] = xc_ref[...]
    xpad[SUBLANES + tc:2 * SUBLANES + tc, :] = jnp.where(ci == nchunks - 1, 0.0, xn_ref[...])

    lam = -lam_ref[...]
    softplus = jnp.maximum(lam, 0.0) + jnp.log1p(jnp.exp(-jnp.abs(lam)))

    for n in range(LRU_BLOCKS):
        lanes = slice(n * LRU_BLOCK, (n + 1) * LRU_BLOCK)
        xc = cb_ref[:, lanes]
        for k in range(LRU_CONV):
            xc = xc + cw_ref[k:k + 1, lanes] * xpad[pl.ds(SUBLANES - LRU_CONV // 2 + k, tc), lanes]
        u = jnp.dot(xc.astype(BF16), w_ref[n], preferred_element_type=F32)
        r = jax.nn.sigmoid(u[:, :LRU_BLOCK] + bias_ref[0:1, lanes])
        i = jax.nn.sigmoid(u[:, LRU_BLOCK:] + bias_ref[1:2, lanes])
        log_a = -LRU_C * r * softplus[:, lanes]
        a = jnp.exp(log_a)
        mult = jnp.sqrt(-jnp.tanh(log_a) * (a * a + 1.0))
        a_scr[:, lanes] = a
        b_scr[:, lanes] = mult * i * xc

    def group_step(g, h):
        base = pl.multiple_of(g * SUBLANES, SUBLANES)
        for j in range(SUBLANES):
            row = base + (SUBLANES - 1 - j if reverse else j)
            h = a_scr[pl.ds(row, 1), :] * h + b_scr[pl.ds(row, 1), :]
            hs_scr[pl.ds(row, 1), :] = h
        return h

    ngroups = tc // SUBLANES
    if reverse:
        h = lax.fori_loop(0, ngroups, lambda g, h: group_step(ngroups - 1 - g, h), h_scr[...])
    else:
        h = lax.fori_loop(0, ngroups, group_step, h_scr[...])
    h_scr[...] = h

    if reverse:
        o_ref[...] = ((hf_ref[...] + hs_scr[...]) * _gelu_tanh(yl_ref[...])).astype(o_ref.dtype)


def _lru_direction(xlyl, h_fwd, d, conv_w, conv_b, wa, ba, wx, bx, lam, bsz, seq, tc):
    t = bsz * seq
    dl = LRU_BLOCKS * LRU_BLOCK
    reverse = d == 1
    nchunks = seq // tc
    assert seq % tc == 0 and tc % SUBLANES == 0
    hb = tc // SUBLANES

    def chunk(b, c):
        return b * nchunks + (nchunks - 1 - c if reverse else c)

    cur = lambda col: pl.BlockSpec((tc, dl), lambda b, c: (chunk(b, c), col))
    const = lambda shape: pl.BlockSpec(shape, lambda b, c: (0,) * len(shape))
    in_specs = [
        pl.BlockSpec((SUBLANES, dl), lambda b, c: (jnp.maximum(chunk(b, c) * hb - 1, 0), 0)),
        cur(0),
        pl.BlockSpec((SUBLANES, dl), lambda b, c: (jnp.minimum((chunk(b, c) + 1) * hb, t // SUBLANES - 1), 0)),
    ]
    args = [xlyl, xlyl, xlyl]
    if reverse:
        in_specs += [cur(1), cur(0)]
        args += [xlyl, h_fwd]
    w = jnp.concatenate([wa[d], wx[d]], axis=-1).astype(BF16)
    bias = jnp.stack([ba[d], bx[d]]).astype(F32)
    in_specs += [const((LRU_CONV, dl)), const((1, dl)), const((LRU_BLOCKS, LRU_BLOCK, 2 * LRU_BLOCK)),
                 const((2, dl)), const((1, dl))]
    args += [conv_w.astype(F32), conv_b.reshape(1, dl).astype(F32), w, bias, lam[d].reshape(1, dl).astype(F32)]
    scratch = [pltpu.VMEM((tc + 2 * SUBLANES, dl), F32), pltpu.VMEM((tc, dl), F32), pltpu.VMEM((tc, dl), F32)]
    if reverse:
        scratch.append(pltpu.VMEM((tc, dl), F32))
    scratch.append(pltpu.VMEM((1, dl), F32))
    return pl.pallas_call(
        functools.partial(_lru_body, reverse=reverse, nchunks=nchunks, tc=tc),
        grid=(bsz, nchunks),
        in_specs=in_specs,
        out_specs=pl.BlockSpec((tc, dl), lambda b, c: (chunk(b, c), 0)),
        out_shape=jax.ShapeDtypeStruct((t, dl), BF16 if reverse else F32),
        scratch_shapes=scratch,
        compiler_params=_params("parallel", "arbitrary"),
        name="rglru_bwd" if reverse else "rglru_fwd",
    )(*args)


def _merge_body(a_ref, r_ref, wa_ref, wr_ref, ga_ref, gr_ref, bg_ref, o_ref):
    pa = jnp.dot(a_ref[...], wa_ref[...], preferred_element_type=F32)
    pr = jnp.dot(r_ref[...], wr_ref[...], preferred_element_type=F32)
    o = jax.nn.sigmoid(ga_ref[...] + bg_ref[0:1, :]) * pa + jax.nn.sigmoid(gr_ref[...] + bg_ref[1:2, :]) * pr
    o_ref[...] = o.astype(o_ref.dtype)


def _merge(attn, lru, w_oa, w_or, gates, b_gate, tm, tn):
    t, da = attn.shape
    d = w_oa.shape[1]
    nj = d // tn
    return pl.pallas_call(
        _merge_body,
        grid=(t // tm, nj),
        in_specs=[pl.BlockSpec((tm, da), lambda i, j: (i, 0)),
                  pl.BlockSpec((tm, lru.shape[1]), lambda i, j: (i, 0)),
                  pl.BlockSpec((da, tn), lambda i, j: (0, j)),
                  pl.BlockSpec((lru.shape[1], tn), lambda i, j: (0, j)),
                  pl.BlockSpec((tm, tn), lambda i, j: (i, j)),
                  pl.BlockSpec((tm, tn), lambda i, j: (i, j + nj)),
                  pl.BlockSpec((2, tn), lambda i, j: (0, j))],
        out_specs=pl.BlockSpec((tm, tn), lambda i, j: (i, j)),
        out_shape=jax.ShapeDtypeStruct((t, d), BF16),
        compiler_params=_params("parallel", "parallel"),
        name="gated_merge",
    )(attn, lru, w_oa, w_or, gates, gates, b_gate.astype(F32))


def _ffn_gate_body(xp_ref, xc_ref, xn_ref, wg_ref, wv_ref, cw_ref, cb_ref, o_ref, xs, g_st, v_st,
                   *, tiles_per_seq, tm):
    i = pl.program_id(0)
    j = pl.program_id(1)
    nj = pl.num_programs(1) - 1
    si = i % tiles_per_seq
    halo = BF16_ROWS

    def matmuls(slot):
        g_st[slot] = jnp.dot(xs[...], wg_ref[...], preferred_element_type=F32)
        v_st[slot] = jnp.dot(xs[halo:halo + tm, :], wv_ref[...], preferred_element_type=F32)

    def elementwise(slot):
        for r0 in range(0, tm, EW_ROWS):
            conv = cb_ref[...]
            for k in range(FFN_CONV):
                conv = conv + cw_ref[k:k + 1, :] * g_st[slot, pl.ds(halo - FFN_CONV // 2 + k + r0, EW_ROWS), :]
            rows = slice(r0, r0 + EW_ROWS)
            o_ref[rows, :] = (_gelu_tanh(conv) * v_st[slot, rows, :]).astype(o_ref.dtype)

    @pl.when(j == 0)
    def _():
        xs[0:halo, :] = jnp.where(si == 0, jnp.zeros_like(xp_ref), xp_ref[...])
        xs[halo:halo + tm, :] = xc_ref[...]
        xs[halo + tm:2 * halo + tm, :] = jnp.where(si == tiles_per_seq - 1, jnp.zeros_like(xn_ref), xn_ref[...])
        matmuls(0)

    for parity in range(2):
        @pl.when(jnp.logical_and(jnp.logical_and(j > 0, j < nj), j % 2 == parity))
        def _():
            elementwise(1 - parity)
            matmuls(parity)

        @pl.when(jnp.logical_and(j == nj, j % 2 == parity))
        def _():
            elementwise(1 - parity)


def _ffn_gate(hn, w_in, conv_w, conv_b, seq, tm, tn):
    t, d = hn.shape
    dff = w_in.shape[1] // 2
    nj = dff // tn
    hb = tm // BF16_ROWS
    assert seq % tm == 0 and dff % tn == 0 and tm % BF16_ROWS == 0
    chunk = lambda j: jnp.minimum(j, nj - 1)
    done = lambda j: jnp.maximum(j - 1, 0)
    return pl.pallas_call(
        functools.partial(_ffn_gate_body, tiles_per_seq=seq // tm, tm=tm),
        grid=(t // tm, nj + 1),
        in_specs=[pl.BlockSpec((BF16_ROWS, d), lambda i, j: (jnp.maximum(i * hb - 1, 0), 0)),
                  pl.BlockSpec((tm, d), lambda i, j: (i, 0), pipeline_mode=pl.Buffered(1)),
                  pl.BlockSpec((BF16_ROWS, d), lambda i, j: (jnp.minimum((i + 1) * hb, t // BF16_ROWS - 1), 0)),
                  pl.BlockSpec((d, tn), lambda i, j: (0, chunk(j))),
                  pl.BlockSpec((d, tn), lambda i, j: (0, nj + chunk(j))),
                  pl.BlockSpec((FFN_CONV, tn), lambda i, j: (0, done(j))),
                  pl.BlockSpec((1, tn), lambda i, j: (0, done(j)))],
        out_specs=pl.BlockSpec((tm, tn), lambda i, j: (i, done(j))),
        out_shape=jax.ShapeDtypeStruct((t, dff), BF16),
        scratch_shapes=[pltpu.VMEM((tm + 2 * BF16_ROWS, d), BF16), pltpu.VMEM((2, tm + 2 * BF16_ROWS, tn), F32),
                        pltpu.VMEM((2, tm, tn), F32)],
        compiler_params=_params("parallel", "arbitrary"),
        name="ffn_gate",
    )(hn, hn, hn, w_in, w_in, conv_w.astype(F32), conv_b.reshape(1, dff).astype(F32))


def _ffn_down_body(u_ref, w_ref, x_ref, lng_ref, o_ref):
    k = pl.program_id(1)

    @pl.when(k == 0)
    def _():
        o_ref[...] = x_ref[...]

    d = o_ref.shape[1]
    for n0 in range(0, d, FFN_OUT_COLS):
        cols = slice(n0, min(n0 + FFN_OUT_COLS, d))
        o_ref[:, cols] += jnp.dot(u_ref[...], w_ref[:, cols], preferred_element_type=F32)

    @pl.when(k == pl.num_programs(1) - 1)
    def _():
        x = o_ref[...]
        var = jnp.mean(x * x, axis=-1, keepdims=True)
        o_ref[...] = x * lax.rsqrt(var + EPS) * lng_ref[...]


def _ffn_down(u, w_down, x1, ln_g, tm, tk):
    t, dff = u.shape
    d = w_down.shape[1]
    assert t % tm == 0 and dff % tk == 0
    return pl.pallas_call(
        _ffn_down_body,
        grid=(t // tm, dff // tk),
        in_specs=[pl.BlockSpec((tm, tk), lambda i, k: (i, k)),
                  pl.BlockSpec((tk, d), lambda i, k: (k, 0)),
                  pl.BlockSpec((tm, d), lambda i, k: (i, 0), pipeline_mode=pl.Buffered(1)),
                  pl.BlockSpec((1, d), lambda i, k: (0, 0))],
        out_specs=pl.BlockSpec((tm, d), lambda i, k: (i, 0)),
        out_shape=jax.ShapeDtypeStruct((t, d), F32),
        compiler_params=_params("parallel", "arbitrary"),
        name="ffn_down",
    )(u, w_down, x1, ln_g.reshape(1, d).astype(F32))


def _tile(n, pref):
    return pref if n % pref == 0 else n


def _trunk(x, w, bias_tbl):
    bsz, seq, d = x.shape
    t = bsz * seq
    x = x.reshape(t, d)
    da = N_HEADS * HEAD_DIM
    dl = LRU_BLOCKS * LRU_BLOCK
    tm = _tile(t, 1024)

    xn = _rmsnorm(x, w["ln_mix_g"], BF16, _tile(t, 256))
    qkv = _matmul(xn, w["w_in"], 0, 3 * da, BF16, tm, 1024, name="in_proj_qkv")
    xlyl = _matmul(xn, w["w_in"], 3 * da, 2 * dl, F32, tm, 1024, name="in_proj_lru")
    gates = _matmul(xn, w["w_in"], 3 * da + 2 * dl, 2 * d, F32, tm, 1024, name="in_proj_gates")

    attn = _attention(qkv, bias_tbl, bsz, seq)

    lru_args = (w["lru_conv_w"], w["lru_conv_b"], w["lru_wa"], w["lru_ba"], w["lru_wx"], w["lru_bx"],
                w["lru_lambda"], bsz, seq, _tile(seq, 256))
    h_fwd = _lru_direction(xlyl, None, 0, *lru_args)
    lru = _lru_direction(xlyl, h_fwd, 1, *lru_args)

    merged = _merge(attn, lru, w["w_o_attn"], w["w_o_lru"], gates, w["b_gate"], tm, 512)
    x1 = _matmul(merged, w["w_out"], 0, d, F32, tm, 512, residual=x, name="out_proj")

    hn = _rmsnorm(x1, w["ln_ffn_g"], BF16, _tile(t, 256))
    u = _ffn_gate(hn, w["w_ffn_in"], w["ffn_conv_w"], w["ffn_conv_b"], seq, _tile(seq, 1024), 512)
    y = _ffn_down(u, w["w_ffn_out"], x1, w["ln_final_g"], _tile(t, 512), 1024)
    return y.reshape(bsz, seq, d)


def kernel(x_prompt, x_sample, ln_mix_g, w_in, b_gate, rpb, lru_conv_w, lru_conv_b, lru_wa, lru_ba, lru_wx,
           lru_bx, lru_lambda, w_o_attn, w_o_lru, w_out, ln_ffn_g, w_ffn_in, ffn_conv_w, ffn_conv_b, w_ffn_out,
           ln_final_g):
    assert ln_mix_g.shape[0] == 1, "single-layer block"
    w = dict(
        ln_mix_g=ln_mix_g[0].astype(F32), w_in=w_in[0].astype(BF16), b_gate=b_gate[0],
        lru_conv_w=lru_conv_w[0], lru_conv_b=lru_conv_b[0], lru_wa=lru_wa[0], lru_ba=lru_ba[0],
        lru_wx=lru_wx[0], lru_bx=lru_bx[0], lru_lambda=lru_lambda[0],
        w_o_attn=w_o_attn[0].astype(BF16), w_o_lru=w_o_lru[0].astype(BF16), w_out=w_out[0].astype(BF16),
        ln_ffn_g=ln_ffn_g[0].astype(F32), w_ffn_in=w_ffn_in[0].astype(BF16),
        ffn_conv_w=ffn_conv_w[0], ffn_conv_b=ffn_conv_b[0], w_ffn_out=w_ffn_out[0].astype(BF16),
        ln_final_g=ln_final_g.astype(F32),
    )
    bias_tbl = _attn_bias_table(rpb[0])
    return (_trunk(x_prompt, w, bias_tbl), _trunk(x_sample, w, bias_tbl))
```

```python
import functools

import jax
import jax.numpy as jnp
import numpy as np
from jax import lax
from jax.experimental import pallas as pl
from jax.experimental.pallas import tpu as pltpu

EPS = 1e-6
GRID_W = 64
N_HEADS = 16
HEAD_DIM = 128
WIN_R = 8
WIN_C = 16
ATTN_GROUP = 8
LRU_BLOCKS = 16
LRU_BLOCK = 128
LRU_CONV = 4
LRU_C = 8.0
FFN_CONV = 3
FFN_OUT_COLS = 1024
MASK_VALUE = -1e30

SUBLANES = 8
BF16_ROWS = 16
EW_ROWS = 16
VMEM_LIMIT_BYTES = 56 * 1024 * 1024

F32 = jnp.float32
BF16 = jnp.bfloat16


def _params(*semantics):
    return pltpu.CompilerParams(dimension_semantics=semantics, vmem_limit_bytes=VMEM_LIMIT_BYTES)


def _gelu_tanh(x):
    c = np.float32(np.sqrt(2.0 / np.pi))
    h = 0.5 * x
    return h + h * jnp.tanh(x * (c + np.float32(c * 0.044715) * (x * x)))


def _sigmoid(x):
    return 0.5 * jnp.tanh(0.5 * x) + 0.5


def _rmsnorm_body(x_ref, g_ref, o_ref):
    x = x_ref[...]
    var = jnp.mean(x * x, axis=-1, keepdims=True)
    o_ref[...] = (x * lax.rsqrt(var + EPS) * g_ref[...]).astype(o_ref.dtype)


def _rmsnorm(x, g, out_dtype, tm):
    t, d = x.shape
    return pl.pallas_call(
        _rmsnorm_body,
        grid=(t // tm,),
        in_specs=[pl.BlockSpec((tm, d), lambda i: (i, 0)), pl.BlockSpec((1, d), lambda i: (0, 0))],
        out_specs=pl.BlockSpec((tm, d), lambda i: (i, 0)),
        out_shape=jax.ShapeDtypeStruct((t, d), out_dtype),
        compiler_params=_params("parallel"),
        name="rmsnorm",
    )(x, g.reshape(1, d))


def _matmul_body(x_ref, w_ref, o_ref):
    o_ref[...] = jnp.dot(x_ref[...], w_ref[...], preferred_element_type=F32).astype(o_ref.dtype)


def _matmul_res_body(x_ref, w_ref, r_ref, o_ref):
    o_ref[...] = r_ref[...] + jnp.dot(x_ref[...], w_ref[...], preferred_element_type=F32)


def _matmul(x, w, col0, ncols, out_dtype, tm, tn, residual=None, name="matmul"):
    t, k = x.shape
    assert w.shape[0] == k and col0 % tn == 0 and ncols % tn == 0 and t % tm == 0
    jb = col0 // tn
    in_specs = [pl.BlockSpec((tm, k), lambda i, j: (i, 0)), pl.BlockSpec((k, tn), lambda i, j: (0, j + jb))]
    args = [x, w]
    body = _matmul_body
    if residual is not None:
        in_specs.append(pl.BlockSpec((tm, tn), lambda i, j: (i, j)))
        args.append(residual)
        body = _matmul_res_body
    return pl.pallas_call(
        body,
        grid=(t // tm, ncols // tn),
        in_specs=in_specs,
        out_specs=pl.BlockSpec((tm, tn), lambda i, j: (i, j)),
        out_shape=jax.ShapeDtypeStruct((t, ncols), out_dtype),
        compiler_params=_params("parallel", "parallel"),
        name=name,
    )(*args)


def _attn_bias_table(rpb):
    cols = np.arange(GRID_W)
    col_start = np.clip(cols - WIN_C // 2, 0, GRID_W - WIN_C)
    key = np.arange(GRID_W)[None, :]
    valid = (key >= col_start[:, None]) & (key < col_start[:, None] + WIN_C)
    dc = np.clip(key - cols[:, None] + (WIN_C - 1), 0, 2 * WIN_C - 2)
    g = rpb.astype(F32)[:, :, dc]
    g = jnp.where(valid[None, None], g, MASK_VALUE)
    tbl = jnp.stack([g[:, v:v + WIN_R] for v in range(WIN_R)], axis=1)
    tbl = jnp.transpose(tbl, (0, 1, 3, 2, 4))
    return tbl.reshape(N_HEADS, WIN_R, GRID_W, WIN_R * GRID_W)


def _attn_body(q_ref, k_ref, v_ref, bias_ref, o_ref, s_scr, p_scr, *, rows):
    scale = HEAD_DIM ** -0.5
    nkeys = WIN_R * GRID_W
    ngroups = rows // ATTN_GROUP

    def window(g, j):
        r = g * ATTN_GROUP + j
        rs = jnp.clip(r - WIN_R // 2, 0, rows - WIN_R)
        return pl.multiple_of(r * GRID_W, GRID_W), pl.multiple_of(rs * GRID_W, GRID_W), rs - r + (WIN_R - 1)

    def scores(g):
        for j in range(ATTN_GROUP):
            q0, k0, _ = window(g, j)
            s_scr[j] = lax.dot_general(q_ref[pl.ds(q0, GRID_W), :], k_ref[pl.ds(k0, nkeys), :],
                                       (((1,), (1,)), ((), ())), preferred_element_type=F32)

    def softmax(g):
        for j in range(ATTN_GROUP):
            _, _, dr0 = window(g, j)
            s = s_scr[j] * scale + bias_ref[0, dr0]
            p = jnp.exp(s - jnp.max(s, axis=-1, keepdims=True))
            p = p / jnp.sum(p, axis=-1, keepdims=True)
            p_scr[j] = p.astype(BF16)

    def values(g):
        for j in range(ATTN_GROUP):
            q0, k0, _ = window(g, j)
            o = jnp.dot(p_scr[j], v_ref[pl.ds(k0, nkeys), :], preferred_element_type=F32)
            o_ref[pl.ds(q0, GRID_W), :] = o.astype(o_ref.dtype)

    scores(0)
    softmax(0)
    scores(1)

    def steady(g, carry):
        values(g - 2)
        softmax(g - 1)
        scores(g)
        return carry

    lax.fori_loop(2, ngroups, steady, 0)
    values(ngroups - 2)
    softmax(ngroups - 1)
    values(ngroups - 1)


def _attention(qkv, bias_tbl, bsz, seq):
    rows = seq // GRID_W
    nkeys = WIN_R * GRID_W
    assert seq % GRID_W == 0 and rows % ATTN_GROUP == 0 and rows // ATTN_GROUP >= 2 and rows >= WIN_R
    blk = lambda part: pl.BlockSpec((seq, HEAD_DIM), lambda b, h: (b, part * N_HEADS + h))
    return pl.pallas_call(
        functools.partial(_attn_body, rows=rows),
        grid=(bsz, N_HEADS),
        in_specs=[blk(0), blk(1), blk(2),
                  pl.BlockSpec((1, WIN_R, GRID_W, nkeys), lambda b, h: (h, 0, 0, 0))],
        out_specs=pl.BlockSpec((seq, HEAD_DIM), lambda b, h: (b, h)),
        out_shape=jax.ShapeDtypeStruct((bsz * seq, N_HEADS * HEAD_DIM), BF16),
        scratch_shapes=[pltpu.VMEM((ATTN_GROUP, GRID_W, nkeys), F32), pltpu.VMEM((ATTN_GROUP, GRID_W, nkeys), BF16)],
        compiler_params=_params("parallel", "parallel"),
        name="natten",
    )(qkv, qkv, qkv, bias_tbl)


def _lru_gates(x_block, n, w_ref, bias_ref, softplus, a_scr, b_scr):
    lanes = slice(n * LRU_BLOCK, (n + 1) * LRU_BLOCK)
    u = jnp.dot(x_block.astype(BF16), w_ref[n], preferred_element_type=F32)
    r = _sigmoid(u[:, :LRU_BLOCK] + bias_ref[0:1, lanes])
    i = _sigmoid(u[:, LRU_BLOCK:] + bias_ref[1:2, lanes])
    log_a = -LRU_C * r * softplus[:, lanes]
    a = jnp.exp(log_a)
    z = -jnp.tanh(log_a) * (a * a + 1.0)
    mult = jnp.where(z > 0.0, z * lax.rsqrt(z), 0.0)
    a_scr[:, lanes] = a
    b_scr[:, lanes] = mult * i * x_block


def _lru_softplus(lam_ref):
    lam = -lam_ref[...]
    return jnp.maximum(lam, 0.0) + jnp.log1p(jnp.exp(-jnp.abs(lam)))


def _lru_scan(a_scr, b_scr, hs_ref, h_scr, tc, reverse):
    ngroups = tc // SUBLANES

    def group_step(g, h):
        base = pl.multiple_of((ngroups - 1 - g if reverse else g) * SUBLANES, SUBLANES)
        for j in range(SUBLANES):
            row = base + (SUBLANES - 1 - j if reverse else j)
            h = a_scr[pl.ds(row, 1), :] * h + b_scr[pl.ds(row, 1), :]
            hs_ref[pl.ds(row, 1), :] = h
        return h

    h_scr[...] = lax.fori_loop(0, ngroups, group_step, h_scr[...])


def _lru_fwd_body(xp_ref, xc_ref, xn_ref, cw_ref, cb_ref, w_ref, bias_ref, lam_ref, h_ref, conv_ref,
                  xpad, a_scr, b_scr, h_scr, *, nchunks, tc):
    c = pl.program_id(1)

    @pl.when(c == 0)
    def _():
        h_scr[...] = jnp.zeros_like(h_scr)

    xpad[0:SUBLANES, :] = jnp.where(c == 0, 0.0, xp_ref[...])
    xpad[SUBLANES:SUBLANES + tc, :] = xc_ref[...]
    xpad[SUBLANES + tc:2 * SUBLANES + tc, :] = jnp.where(c == nchunks - 1, 0.0, xn_ref[...])
    softplus = _lru_softplus(lam_ref)
    for n in range(LRU_BLOCKS):
        lanes = slice(n * LRU_BLOCK, (n + 1) * LRU_BLOCK)
        xc = cb_ref[:, lanes]
        for k in range(LRU_CONV):
            xc = xc + cw_ref[k:k + 1, lanes] * xpad[pl.ds(SUBLANES - LRU_CONV // 2 + k, tc), lanes]
        conv_ref[:, lanes] = xc
        _lru_gates(xc, n, w_ref, bias_ref, softplus, a_scr, b_scr)
    _lru_scan(a_scr, b_scr, h_ref, h_scr, tc, reverse=False)


def _lru_bwd_body(conv_ref, yl_ref, hf_ref, w_ref, bias_ref, lam_ref, o_ref, a_scr, b_scr, hs_scr, h_scr, *, tc):
    @pl.when(pl.program_id(1) == 0)
    def _():
        h_scr[...] = jnp.zeros_like(h_scr)

    softplus = _lru_softplus(lam_ref)
    for n in range(LRU_BLOCKS):
        _lru_gates(conv_ref[:, n * LRU_BLOCK:(n + 1) * LRU_BLOCK], n, w_ref, bias_ref, softplus, a_scr, b_scr)
    _lru_scan(a_scr, b_scr, hs_scr, h_scr, tc, reverse=True)
    o_ref[...] = ((hf_ref[...] + hs_scr[...]) * _gelu_tanh(yl_ref[...])).astype(o_ref.dtype)


def _lru(xlyl, conv_w, conv_b, wa, ba, wx, bx, lam, bsz, seq, tc):
    t = bsz * seq
    dl = LRU_BLOCKS * LRU_BLOCK
    nchunks = seq // tc
    assert seq % tc == 0 and tc % SUBLANES == 0
    hb = tc // SUBLANES
    const = lambda shape: pl.BlockSpec(shape, lambda b, c: (0,) * len(shape))
    gate_specs = [const((LRU_BLOCKS, LRU_BLOCK, 2 * LRU_BLOCK)), const((2, dl)), const((1, dl))]

    def gate_args(d):
        return [jnp.concatenate([wa[d], wx[d]], axis=-1).astype(BF16),
                jnp.stack([ba[d], bx[d]]).astype(F32), lam[d].reshape(1, dl).astype(F32)]

    fwd = lambda b, c: b * nchunks + c
    bwd = lambda b, c: b * nchunks + nchunks - 1 - c
    h_fwd, conv = pl.pallas_call(
        functools.partial(_lru_fwd_body, nchunks=nchunks, tc=tc),
        grid=(bsz, nchunks),
        in_specs=[pl.BlockSpec((SUBLANES, dl), lambda b, c: (jnp.maximum(fwd(b, c) * hb - 1, 0), 0)),
                  pl.BlockSpec((tc, dl), lambda b, c: (fwd(b, c), 0)),
                  pl.BlockSpec((SUBLANES, dl), lambda b, c: (jnp.minimum((fwd(b, c) + 1) * hb, t // SUBLANES - 1), 0)),
                  const((LRU_CONV, dl)), const((1, dl))] + gate_specs,
        out_specs=[pl.BlockSpec((tc, dl), lambda b, c: (fwd(b, c), 0))] * 2,
        out_shape=[jax.ShapeDtypeStruct((t, dl), F32)] * 2,
        scratch_shapes=[pltpu.VMEM((tc + 2 * SUBLANES, dl), F32), pltpu.VMEM((tc, dl), F32),
                        pltpu.VMEM((tc, dl), F32), pltpu.VMEM((1, dl), F32)],
        compiler_params=_params("parallel", "arbitrary"),
        name="rglru_fwd",
    )(xlyl, xlyl, xlyl, conv_w.astype(F32), conv_b.reshape(1, dl).astype(F32), *gate_args(0))
    return pl.pallas_call(
        functools.partial(_lru_bwd_body, tc=tc),
        grid=(bsz, nchunks),
        in_specs=[pl.BlockSpec((tc, dl), lambda b, c: (bwd(b, c), 0)),
                  pl.BlockSpec((tc, dl), lambda b, c: (bwd(b, c), 1)),
                  pl.BlockSpec((tc, dl), lambda b, c: (bwd(b, c), 0))] + gate_specs,
        out_specs=pl.BlockSpec((tc, dl), lambda b, c: (bwd(b, c), 0)),
        out_shape=jax.ShapeDtypeStruct((t, dl), BF16),
        scratch_shapes=[pltpu.VMEM((tc, dl), F32), pltpu.VMEM((tc, dl), F32), pltpu.VMEM((tc, dl), F32),
                        pltpu.VMEM((1, dl), F32)],
        compiler_params=_params("parallel", "arbitrary"),
        name="rglru_bwd",
    )(conv, xlyl, h_fwd, *gate_args(1))


def _merge_body(a_ref, r_ref, wa_ref, wr_ref, ga_ref, gr_ref, bg_ref, o_ref):
    pa = jnp.dot(a_ref[...], wa_ref[...], preferred_element_type=F32)
    pr = jnp.dot(r_ref[...], wr_ref[...], preferred_element_type=F32)
    o = _sigmoid(ga_ref[...] + bg_ref[0:1, :]) * pa + _sigmoid(gr_ref[...] + bg_ref[1:2, :]) * pr
    o_ref[...] = o.astype(o_ref.dtype)


def _merge(attn, lru, w_oa, w_or, gates, b_gate, tm, tn):
    t, da = attn.shape
    d = w_oa.shape[1]
    nj = d // tn
    return pl.pallas_call(
        _merge_body,
        grid=(t // tm, nj),
        in_specs=[pl.BlockSpec((tm, da), lambda i, j: (i, 0)),
                  pl.BlockSpec((tm, lru.shape[1]), lambda i, j: (i, 0)),
                  pl.BlockSpec((da, tn), lambda i, j: (0, j)),
                  pl.BlockSpec((lru.shape[1], tn), lambda i, j: (0, j)),
                  pl.BlockSpec((tm, tn), lambda i, j: (i, j)),
                  pl.BlockSpec((tm, tn), lambda i, j: (i, j + nj)),
                  pl.BlockSpec((2, tn), lambda i, j: (0, j))],
        out_specs=pl.BlockSpec((tm, tn), lambda i, j: (i, j)),
        out_shape=jax.ShapeDtypeStruct((t, d), BF16),
        compiler_params=_params("parallel", "parallel"),
        name="gated_merge",
    )(attn, lru, w_oa, w_or, gates, gates, b_gate.astype(F32))


def _ffn_gate_body(xp_ref, xc_ref, xn_ref, wg_ref, wv_ref, cw_ref, o_ref, xs, g_st, v_st,
                   *, tiles_per_seq, tm):
    i = pl.program_id(0)
    j = pl.program_id(1)
    nj = pl.num_programs(1) - 1
    si = i % tiles_per_seq
    halo = BF16_ROWS

    def matmuls(slot):
        g_st[slot] = jnp.dot(xs[...], wg_ref[...], preferred_element_type=F32)
        v_st[slot] = jnp.dot(xs[halo:halo + tm, :], wv_ref[...], preferred_element_type=F32)

    def elementwise(slot):
        for r0 in range(0, tm, EW_ROWS):
            conv = cw_ref[FFN_CONV]
            for k in range(FFN_CONV):
                conv = conv + cw_ref[k] * g_st[slot, pl.ds(halo - FFN_CONV // 2 + k + r0, EW_ROWS), :]
            rows = slice(r0, r0 + EW_ROWS)
            o_ref[rows, :] = (_gelu_tanh(conv) * v_st[slot, rows, :]).astype(o_ref.dtype)

    @pl.when(j == 0)
    def _():
        xs[0:halo, :] = jnp.where(si == 0, jnp.zeros_like(xp_ref), xp_ref[...])
        xs[halo:halo + tm, :] = xc_ref[...]
        xs[halo + tm:2 * halo + tm, :] = jnp.where(si == tiles_per_seq - 1, jnp.zeros_like(xn_ref), xn_ref[...])
        matmuls(0)

    for parity in range(2):
        @pl.when(jnp.logical_and(jnp.logical_and(j > 0, j < nj), j % 2 == parity))
        def _():
            elementwise(1 - parity)
            matmuls(parity)

        @pl.when(jnp.logical_and(j == nj, j % 2 == parity))
        def _():
            elementwise(1 - parity)


def _ffn_gate(hn, w_in, conv_w, conv_b, seq, tm, tn):
    t, d = hn.shape
    dff = w_in.shape[1] // 2
    nj = dff // tn
    hb = tm // BF16_ROWS
    assert seq % tm == 0 and dff % tn == 0 and tm % BF16_ROWS == 0
    taps = jnp.concatenate([conv_w.astype(F32), conv_b.reshape(1, dff).astype(F32)])
    taps = jnp.broadcast_to(taps[:, None, :], (FFN_CONV + 1, EW_ROWS, dff))
    chunk = lambda j: jnp.minimum(j, nj - 1)
    done = lambda j: jnp.maximum(j - 1, 0)
    return pl.pallas_call(
        functools.partial(_ffn_gate_body, tiles_per_seq=seq // tm, tm=tm),
        grid=(t // tm, nj + 1),
        in_specs=[pl.BlockSpec((BF16_ROWS, d), lambda i, j: (jnp.maximum(i * hb - 1, 0), 0)),
                  pl.BlockSpec((tm, d), lambda i, j: (i, 0), pipeline_mode=pl.Buffered(1)),
                  pl.BlockSpec((BF16_ROWS, d), lambda i, j: (jnp.minimum((i + 1) * hb, t // BF16_ROWS - 1), 0)),
                  pl.BlockSpec((d, tn), lambda i, j: (0, chunk(j))),
                  pl.BlockSpec((d, tn), lambda i, j: (0, nj + chunk(j))),
                  pl.BlockSpec((FFN_CONV + 1, EW_ROWS, tn), lambda i, j: (0, 0, done(j)))],
        out_specs=pl.BlockSpec((tm, tn), lambda i, j: (i, done(j))),
        out_shape=jax.ShapeDtypeStruct((t, dff), BF16),
        scratch_shapes=[pltpu.VMEM((tm + 2 * BF16_ROWS, d), BF16), pltpu.VMEM((2, tm + 2 * BF16_ROWS, tn), F32),
                        pltpu.VMEM((2, tm, tn), F32)],
        compiler_params=_params("parallel", "arbitrary"),
        name="ffn_gate",
    )(hn, hn, hn, w_in, w_in, taps)


def _ffn_down_body(u_ref, w_ref, x_ref, lng_ref, o_ref):
    k = pl.program_id(1)

    @pl.when(k == 0)
    def _():
        o_ref[...] = x_ref[...]

    d = o_ref.shape[1]
    for n0 in range(0, d, FFN_OUT_COLS):
        cols = slice(n0, min(n0 + FFN_OUT_COLS, d))
        o_ref[:, cols] += jnp.dot(u_ref[...], w_ref[:, cols], preferred_element_type=F32)

    @pl.when(k == pl.num_programs(1) - 1)
    def _():
        x = o_ref[...]
        var = jnp.mean(x * x, axis=-1, keepdims=True)
        o_ref[...] = x * lax.rsqrt(var + EPS) * lng_ref[...]


def _ffn_down(u, w_down, x1, ln_g, tm, tk):
    t, dff = u.shape
    d = w_down.shape[1]
    assert t % tm == 0 and dff % tk == 0
    return pl.pallas_call(
        _ffn_down_body,
        grid=(t // tm, dff // tk),
        in_specs=[pl.BlockSpec((tm, tk), lambda i, k: (i, k)),
                  pl.BlockSpec((tk, d), lambda i, k: (k, 0)),
                  pl.BlockSpec((tm, d), lambda i, k: (i, 0), pipeline_mode=pl.Buffered(1)),
                  pl.BlockSpec((1, d), lambda i, k: (0, 0))],
        out_specs=pl.BlockSpec((tm, d), lambda i, k: (i, 0)),
        out_shape=jax.ShapeDtypeStruct((t, d), F32),
        compiler_params=_params("parallel", "arbitrary"),
        name="ffn_down",
    )(u, w_down, x1, ln_g.reshape(1, d).astype(F32))


def _tile(n, pref):
    return pref if n % pref == 0 else n


def _trunk(x, w, bias_tbl):
    bsz, seq, d = x.shape
    t = bsz * seq
    x = x.reshape(t, d)
    da = N_HEADS * HEAD_DIM
    dl = LRU_BLOCKS * LRU_BLOCK
    tm = _tile(t, 1024)

    xn = _rmsnorm(x, w["ln_mix_g"], BF16, _tile(t, 256))
    qkv = _matmul(xn, w["w_in"], 0, 3 * da, BF16, tm, 1024, name="in_proj_qkv")
    xlyl = _matmul(xn, w["w_in"], 3 * da, 2 * dl, F32, tm, 1024, name="in_proj_lru")
    gates = _matmul(xn, w["w_in"], 3 * da + 2 * dl, 2 * d, F32, tm, 1024, name="in_proj_gates")

    attn = _attention(qkv, bias_tbl, bsz, seq)

    lru = _lru(xlyl, w["lru_conv_w"], w["lru_conv_b"], w["lru_wa"], w["lru_ba"], w["lru_wx"], w["lru_bx"],
               w["lru_lambda"], bsz, seq, _tile(seq, 256))

    merged = _merge(attn, lru, w["w_o_attn"], w["w_o_lru"], gates, w["b_gate"], tm, 512)
    x1 = _matmul(merged, w["w_out"], 0, d, F32, tm, 512, residual=x, name="out_proj")

    hn = _rmsnorm(x1, w["ln_ffn_g"], BF16, _tile(t, 256))
    u = _ffn_gate(hn, w["w_ffn_in"], w["ffn_conv_w"], w["ffn_conv_b"], seq, _tile(seq, 1024), 512)
    y = _ffn_down(u, w["w_ffn_out"], x1, w["ln_final_g"], _tile(t, 512), 1024)
    return y.reshape(bsz, seq, d)


def kernel(x_prompt, x_sample, ln_mix_g, w_in, b_gate, rpb, lru_conv_w, lru_conv_b, lru_wa, lru_ba, lru_wx,
           lru_bx, lru_lambda, w_o_attn, w_o_lru, w_out, ln_ffn_g, w_ffn_in, ffn_conv_w, ffn_conv_b, w_ffn_out,
           ln_final_g):
    assert ln_mix_g.shape[0] == 1, "single-layer block"
    w = dict(
        ln_mix_g=ln_mix_g[0].astype(F32), w_in=w_in[0].astype(BF16), b_gate=b_gate[0],
        lru_conv_w=lru_conv_w[0], lru_conv_b=lru_conv_b[0], lru_wa=lru_wa[0], lru_ba=lru_ba[0],
        lru_wx=lru_wx[0], lru_bx=lru_bx[0], lru_lambda=lru_lambda[0],
        w_o_attn=w_o_attn[0].astype(BF16), w_o_lru=w_o_lru[0].astype(BF16), w_out=w_out[0].astype(BF16),
        ln_ffn_g=ln_ffn_g[0].astype(F32), w_ffn_in=w_ffn_in[0].astype(BF16),
        ffn_conv_w=ffn_conv_w[0], ffn_conv_b=ffn_conv_b[0], w_ffn_out=w_ffn_out[0].astype(BF16),
        ln_final_g=ln_final_g.astype(F32),
    )
    bias_tbl = _attn_bias_table(rpb[0])
    return (_trunk(x_prompt, w, bias_tbl), _trunk(x_sample, w, bias_tbl))
```

```python
import functools

import jax
import jax.numpy as jnp
import numpy as np
from jax import lax
from jax.experimental import pallas as pl
from jax.experimental.pallas import tpu as pltpu

EPS = 1e-6
GRID_W = 64
N_HEADS = 16
HEAD_DIM = 128
WIN_R = 8
WIN_C = 16
ATTN_GROUP = 8
LRU_BLOCKS = 16
LRU_BLOCK = 128
LRU_CONV = 4
LRU_C = 8.0
FFN_CONV = 3
FFN_OUT_COLS = 1024
MASK_VALUE = -1e30

SUBLANES = 8
BF16_ROWS = 16
EW_ROWS = 16
VMEM_LIMIT_BYTES = 56 * 1024 * 1024

F32 = jnp.float32
BF16 = jnp.bfloat16


def _params(*semantics):
    return pltpu.CompilerParams(dimension_semantics=semantics, vmem_limit_bytes=VMEM_LIMIT_BYTES)


def _gelu_tanh(x):
    c = np.float32(np.sqrt(2.0 / np.pi))
    h = 0.5 * x
    return h + h * jnp.tanh(x * (c + np.float32(c * 0.044715) * (x * x)))


def _sigmoid(x):
    return 0.5 * jnp.tanh(0.5 * x) + 0.5


def _rmsnorm_body(x_ref, g_ref, o_ref):
    x = x_ref[...]
    var = jnp.mean(x * x, axis=-1, keepdims=True)
    o_ref[...] = (x * lax.rsqrt(var + EPS) * g_ref[...]).astype(o_ref.dtype)


def _rmsnorm(x, g, out_dtype, tm):
    t, d = x.shape
    return pl.pallas_call(
        _rmsnorm_body,
        grid=(t // tm,),
        in_specs=[pl.BlockSpec((tm, d), lambda i: (i, 0)), pl.BlockSpec((1, d), lambda i: (0, 0))],
        out_specs=pl.BlockSpec((tm, d), lambda i: (i, 0)),
        out_shape=jax.ShapeDtypeStruct((t, d), out_dtype),
        compiler_params=_params("parallel"),
        name="rmsnorm",
    )(x, g.reshape(1, d))


def _matmul_body(x_ref, w_ref, o_ref):
    o_ref[...] = jnp.dot(x_ref[...], w_ref[...], preferred_element_type=F32).astype(o_ref.dtype)


def _matmul_res_body(x_ref, w_ref, r_ref, o_ref):
    o_ref[...] = r_ref[...] + jnp.dot(x_ref[...], w_ref[...], preferred_element_type=F32)


def _matmul(x, w, col0, ncols, out_dtype, tm, tn, residual=None, name="matmul"):
    t, k = x.shape
    assert w.shape[0] == k and col0 % tn == 0 and ncols % tn == 0 and t % tm == 0
    jb = col0 // tn
    in_specs = [pl.BlockSpec((tm, k), lambda i, j: (i, 0)), pl.BlockSpec((k, tn), lambda i, j: (0, j + jb))]
    args = [x, w]
    body = _matmul_body
    if residual is not None:
        in_specs.append(pl.BlockSpec((tm, tn), lambda i, j: (i, j)))
        args.append(residual)
        body = _matmul_res_body
    return pl.pallas_call(
        body,
        grid=(t // tm, ncols // tn),
        in_specs=in_specs,
        out_specs=pl.BlockSpec((tm, tn), lambda i, j: (i, j)),
        out_shape=jax.ShapeDtypeStruct((t, ncols), out_dtype),
        compiler_params=_params("parallel", "parallel"),
        name=name,
    )(*args)


def _attn_bias_table(rpb):
    cols = np.arange(GRID_W)
    col_start = np.clip(cols - WIN_C // 2, 0, GRID_W - WIN_C)
    key = np.arange(GRID_W)[None, :]
    valid = (key >= col_start[:, None]) & (key < col_start[:, None] + WIN_C)
    dc = np.clip(key - cols[:, None] + (WIN_C - 1), 0, 2 * WIN_C - 2)
    g = rpb.astype(F32)[:, :, dc]
    g = jnp.where(valid[None, None], g, MASK_VALUE)
    tbl = jnp.stack([g[:, v:v + WIN_R] for v in range(WIN_R)], axis=1)
    tbl = jnp.transpose(tbl, (0, 1, 3, 2, 4))
    return tbl.reshape(N_HEADS, WIN_R, GRID_W, WIN_R * GRID_W)


def _attn_body(q_ref, k_ref, v_ref, bias_ref, o_ref, s_scr, p_scr, *, rows):
    scale = HEAD_DIM ** -0.5
    nkeys = WIN_R * GRID_W
    ngroups = rows // ATTN_GROUP

    def window(g, j):
        r = g * ATTN_GROUP + j
        rs = jnp.clip(r - WIN_R // 2, 0, rows - WIN_R)
        return pl.multiple_of(r * GRID_W, GRID_W), pl.multiple_of(rs * GRID_W, GRID_W), rs - r + (WIN_R - 1)

    def scores(g):
        for j in range(ATTN_GROUP):
            q0, k0, _ = window(g, j)
            s_scr[j] = lax.dot_general(q_ref[pl.ds(q0, GRID_W), :], k_ref[pl.ds(k0, nkeys), :],
                                       (((1,), (1,)), ((), ())), preferred_element_type=F32)

    def softmax(g):
        for j in range(ATTN_GROUP):
            _, _, dr0 = window(g, j)
            s = s_scr[j] * scale + bias_ref[0, dr0]
            p = jnp.exp(s - jnp.max(s, axis=-1, keepdims=True))
            p = p / jnp.sum(p, axis=-1, keepdims=True)
            p_scr[j] = p.astype(BF16)

    def values(g):
        for j in range(ATTN_GROUP):
            q0, k0, _ = window(g, j)
            o = jnp.dot(p_scr[j], v_ref[pl.ds(k0, nkeys), :], preferred_element_type=F32)
            o_ref[pl.ds(q0, GRID_W), :] = o.astype(o_ref.dtype)

    scores(0)
    softmax(0)
    scores(1)

    def steady(g, carry):
        values(g - 2)
        softmax(g - 1)
        scores(g)
        return carry

    lax.fori_loop(2, ngroups, steady, 0)
    values(ngroups - 2)
    softmax(ngroups - 1)
    values(ngroups - 1)


def _attention(qkv, bias_tbl, bsz, seq):
    rows = seq // GRID_W
    nkeys = WIN_R * GRID_W
    assert seq % GRID_W == 0 and rows % ATTN_GROUP == 0 and rows // ATTN_GROUP >= 2 and rows >= WIN_R
    blk = lambda part: pl.BlockSpec((seq, HEAD_DIM), lambda b, h: (b, part * N_HEADS + h))
    return pl.pallas_call(
        functools.partial(_attn_body, rows=rows),
        grid=(bsz, N_HEADS),
        in_specs=[blk(0), blk(1), blk(2),
                  pl.BlockSpec((1, WIN_R, GRID_W, nkeys), lambda b, h: (h, 0, 0, 0))],
        out_specs=pl.BlockSpec((seq, HEAD_DIM), lambda b, h: (b, h)),
        out_shape=jax.ShapeDtypeStruct((bsz * seq, N_HEADS * HEAD_DIM), BF16),
        scratch_shapes=[pltpu.VMEM((ATTN_GROUP, GRID_W, nkeys), F32), pltpu.VMEM((ATTN_GROUP, GRID_W, nkeys), BF16)],
        compiler_params=_params("parallel", "parallel"),
        name="natten",
    )(qkv, qkv, qkv, bias_tbl)


def _lru_gates(x_block, n, w_ref, bias_ref, softplus, a_scr, b_scr):
    lanes = slice(n * LRU_BLOCK, (n + 1) * LRU_BLOCK)
    u = jnp.dot(x_block.astype(BF16), w_ref[n], preferred_element_type=F32)
    r = _sigmoid(u[:, :LRU_BLOCK] + bias_ref[0:1, lanes])
    i = _sigmoid(u[:, LRU_BLOCK:] + bias_ref[1:2, lanes])
    log_a = -LRU_C * r * softplus[:, lanes]
    a = jnp.exp(log_a)
    z = -jnp.tanh(log_a) * (a * a + 1.0)
    mult = jnp.where(z > 0.0, z * lax.rsqrt(z), 0.0)
    a_scr[:, lanes] = a
    b_scr[:, lanes] = mult * i * x_block


def _lru_softplus(lam_ref):
    lam = -lam_ref[...]
    return jnp.maximum(lam, 0.0) + jnp.log1p(jnp.exp(-jnp.abs(lam)))


def _lru_scan(a_scr, b_scr, hs_ref, h_scr, tc, reverse):
    ngroups = tc // SUBLANES

    def group_step(g, h):
        base = pl.multiple_of((ngroups - 1 - g if reverse else g) * SUBLANES, SUBLANES)
        for j in range(SUBLANES):
            row = base + (SUBLANES - 1 - j if reverse else j)
            h = a_scr[pl.ds(row, 1), :] * h + b_scr[pl.ds(row, 1), :]
            hs_ref[pl.ds(row, 1), :] = h
        return h

    h_scr[...] = lax.fori_loop(0, ngroups, group_step, h_scr[...])


def _lru_fwd_body(xp_ref, xc_ref, xn_ref, cw_ref, cb_ref, w_ref, bias_ref, lam_ref, h_ref, conv_ref,
                  xpad, a_scr, b_scr, h_scr, *, nchunks, tc):
    c = pl.program_id(1)

    @pl.when(c == 0)
    def _():
        h_scr[...] = jnp.zeros_like(h_scr)

    xpad[0:SUBLANES, :] = jnp.where(c == 0, 0.0, xp_ref[...])
    xpad[SUBLANES:SUBLANES + tc, :] = xc_ref[...]
    xpad[SUBLANES + tc:2 * SUBLANES + tc, :] = jnp.where(c == nchunks - 1, 0.0, xn_ref[...])
    softplus = _lru_softplus(lam_ref)
    for n in range(LRU_BLOCKS):
        lanes = slice(n * LRU_BLOCK, (n + 1) * LRU_BLOCK)
        xc = cb_ref[:, lanes]
        for k in range(LRU_CONV):
            xc = xc + cw_ref[k:k + 1, lanes] * xpad[pl.ds(SUBLANES - LRU_CONV // 2 + k, tc), lanes]
        conv_ref[:, lanes] = xc
        _lru_gates(xc, n, w_ref, bias_ref, softplus, a_scr, b_scr)
    _lru_scan(a_scr, b_scr, h_ref, h_scr, tc, reverse=False)


def _lru_bwd_body(conv_ref, yl_ref, hf_ref, w_ref, bias_ref, lam_ref, o_ref, a_scr, b_scr, hs_scr, h_scr, *, tc):
    @pl.when(pl.program_id(1) == 0)
    def _():
        h_scr[...] = jnp.zeros_like(h_scr)

    softplus = _lru_softplus(lam_ref)
    for n in range(LRU_BLOCKS):
        _lru_gates(conv_ref[:, n * LRU_BLOCK:(n + 1) * LRU_BLOCK], n, w_ref, bias_ref, softplus, a_scr, b_scr)
    _lru_scan(a_scr, b_scr, hs_scr, h_scr, tc, reverse=True)
    o_ref[...] = ((hf_ref[...] + hs_scr[...]) * _gelu_tanh(yl_ref[...])).astype(o_ref.dtype)


def _lru(xlyl, conv_w, conv_b, wa, ba, wx, bx, lam, bsz, seq, tc):
    t = bsz * seq
    dl = LRU_BLOCKS * LRU_BLOCK
    nchunks = seq // tc
    assert seq % tc == 0 and tc % SUBLANES == 0
    hb = tc // SUBLANES
    const = lambda shape: pl.BlockSpec(shape, lambda b, c: (0,) * len(shape))
    gate_specs = [const((LRU_BLOCKS, LRU_BLOCK, 2 * LRU_BLOCK)), const((2, dl)), const((1, dl))]

    def gate_args(d):
        return [jnp.concatenate([wa[d], wx[d]], axis=-1).astype(BF16),
                jnp.stack([ba[d], bx[d]]).astype(F32), lam[d].reshape(1, dl).astype(F32)]

    fwd = lambda b, c: b * nchunks + c
    bwd = lambda b, c: b * nchunks + nchunks - 1 - c
    h_fwd, conv = pl.pallas_call(
        functools.partial(_lru_fwd_body, nchunks=nchunks, tc=tc),
        grid=(bsz, nchunks),
        in_specs=[pl.BlockSpec((SUBLANES, dl), lambda b, c: (jnp.maximum(fwd(b, c) * hb - 1, 0), 0)),
                  pl.BlockSpec((tc, dl), lambda b, c: (fwd(b, c), 0)),
                  pl.BlockSpec((SUBLANES, dl), lambda b, c: (jnp.minimum((fwd(b, c) + 1) * hb, t // SUBLANES - 1), 0)),
                  const((LRU_CONV, dl)), const((1, dl))] + gate_specs,
        out_specs=[pl.BlockSpec((tc, dl), lambda b, c: (fwd(b, c), 0))] * 2,
        out_shape=[jax.ShapeDtypeStruct((t, dl), F32)] * 2,
        scratch_shapes=[pltpu.VMEM((tc + 2 * SUBLANES, dl), F32), pltpu.VMEM((tc, dl), F32),
                        pltpu.VMEM((tc, dl), F32), pltpu.VMEM((1, dl), F32)],
        compiler_params=_params("parallel", "arbitrary"),
        name="rglru_fwd",
    )(xlyl, xlyl, xlyl, conv_w.astype(F32), conv_b.reshape(1, dl).astype(F32), *gate_args(0))
    return pl.pallas_call(
        functools.partial(_lru_bwd_body, tc=tc),
        grid=(bsz, nchunks),
        in_specs=[pl.BlockSpec((tc, dl), lambda b, c: (bwd(b, c), 0)),
                  pl.BlockSpec((tc, dl), lambda b, c: (bwd(b, c), 1)),
                  pl.BlockSpec((tc, dl), lambda b, c: (bwd(b, c), 0))] + gate_specs,
        out_specs=pl.BlockSpec((tc, dl), lambda b, c: (bwd(b, c), 0)),
        out_shape=jax.ShapeDtypeStruct((t, dl), BF16),
        scratch_shapes=[pltpu.VMEM((tc, dl), F32), pltpu.VMEM((tc, dl), F32), pltpu.VMEM((tc, dl), F32),
                        pltpu.VMEM((1, dl), F32)],
        compiler_params=_params("parallel", "arbitrary"),
        name="rglru_bwd",
    )(conv, xlyl, h_fwd, *gate_args(1))


def _merge_body(a_ref, r_ref, wa_ref, wr_ref, ga_ref, gr_ref, bg_ref, o_ref):
    pa = jnp.dot(a_ref[...], wa_ref[...], preferred_element_type=F32)
    pr = jnp.dot(r_ref[...], wr_ref[...], preferred_element_type=F32)
    o = _sigmoid(ga_ref[...] + bg_ref[0:1, :]) * pa + _sigmoid(gr_ref[...] + bg_ref[1:2, :]) * pr
    o_ref[...] = o.astype(o_ref.dtype)


def _merge(attn, lru, w_oa, w_or, gates, b_gate, tm, tn):
    t, da = attn.shape
    d = w_oa.shape[1]
    nj = d // tn
    return pl.pallas_call(
        _merge_body,
        grid=(t // tm, nj),
        in_specs=[pl.BlockSpec((tm, da), lambda i, j: (i, 0)),
                  pl.BlockSpec((tm, lru.shape[1]), lambda i, j: (i, 0)),
                  pl.BlockSpec((da, tn), lambda i, j: (0, j)),
                  pl.BlockSpec((lru.shape[1], tn), lambda i, j: (0, j)),
                  pl.BlockSpec((tm, tn), lambda i, j: (i, j)),
                  pl.BlockSpec((tm, tn), lambda i, j: (i, j + nj)),
                  pl.BlockSpec((2, tn), lambda i, j: (0, j))],
        out_specs=pl.BlockSpec((tm, tn), lambda i, j: (i, j)),
        out_shape=jax.ShapeDtypeStruct((t, d), BF16),
        compiler_params=_params("parallel", "parallel"),
        name="gated_merge",
    )(attn, lru, w_oa, w_or, gates, gates, b_gate.astype(F32))


def _ffn_gate_body(xp_ref, xc_ref, xn_ref, wg_ref, wv_ref, cw_ref, o_ref, xs, g_st, v_st,
                   *, tiles_per_seq, tm):
    i = pl.program_id(0)
    j = pl.program_id(1)
    nj = pl.num_programs(1) - 1
    si = i % tiles_per_seq
    halo = BF16_ROWS

    def matmuls(slot):
        g_st[slot] = jnp.dot(xs[...], wg_ref[...], preferred_element_type=F32)
        v_st[slot] = jnp.dot(xs[halo:halo + tm, :], wv_ref[...], preferred_element_type=F32)

    def elementwise(slot):
        for r0 in range(0, tm, EW_ROWS):
            conv = cw_ref[FFN_CONV]
            for k in range(FFN_CONV):
                conv = conv + cw_ref[k] * g_st[slot, pl.ds(halo - FFN_CONV // 2 + k + r0, EW_ROWS), :]
            rows = slice(r0, r0 + EW_ROWS)
            o_ref[rows, :] = (_gelu_tanh(conv) * v_st[slot, rows, :]).astype(o_ref.dtype)

    @pl.when(j == 0)
    def _():
        xs[0:halo, :] = jnp.where(si == 0, jnp.zeros_like(xp_ref), xp_ref[...])
        xs[halo:halo + tm, :] = xc_ref[...]
        xs[halo + tm:2 * halo + tm, :] = jnp.where(si == tiles_per_seq - 1, jnp.zeros_like(xn_ref), xn_ref[...])
        matmuls(0)

    for parity in range(2):
        @pl.when(jnp.logical_and(jnp.logical_and(j > 0, j < nj), j % 2 == parity))
        def _():
            elementwise(1 - parity)
            matmuls(parity)

        @pl.when(jnp.logical_and(j == nj, j % 2 == parity))
        def _():
            elementwise(1 - parity)


def _ffn_gate(hn, w_in, conv_w, conv_b, seq, tm, tn):
    t, d = hn.shape
    dff = w_in.shape[1] // 2
    nj = dff // tn
    hb = tm // BF16_ROWS
    assert seq % tm == 0 and dff % tn == 0 and tm % BF16_ROWS == 0
    taps = jnp.concatenate([conv_w.astype(F32), conv_b.reshape(1, dff).astype(F32)])
    taps = jnp.broadcast_to(taps[:, None, :], (FFN_CONV + 1, EW_ROWS, dff))
    chunk = lambda j: jnp.minimum(j, nj - 1)
    done = lambda j: jnp.maximum(j - 1, 0)
    return pl.pallas_call(
        functools.partial(_ffn_gate_body, tiles_per_seq=seq // tm, tm=tm),
        grid=(t // tm, nj + 1),
        in_specs=[pl.BlockSpec((BF16_ROWS, d), lambda i, j: (jnp.maximum(i * hb - 1, 0), 0)),
                  pl.BlockSpec((tm, d), lambda i, j: (i, 0), pipeline_mode=pl.Buffered(1)),
                  pl.BlockSpec((BF16_ROWS, d), lambda i, j: (jnp.minimum((i + 1) * hb, t // BF16_ROWS - 1), 0)),
                  pl.BlockSpec((d, tn), lambda i, j: (0, chunk(j))),
                  pl.BlockSpec((d, tn), lambda i, j: (0, nj + chunk(j))),
                  pl.BlockSpec((FFN_CONV + 1, EW_ROWS, tn), lambda i, j: (0, 0, done(j)))],
        out_specs=pl.BlockSpec((tm, tn), lambda i, j: (i, done(j))),
        out_shape=jax.ShapeDtypeStruct((t, dff), BF16),
        scratch_shapes=[pltpu.VMEM((tm + 2 * BF16_ROWS, d), BF16), pltpu.VMEM((2, tm + 2 * BF16_ROWS, tn), F32),
                        pltpu.VMEM((2, tm, tn), F32)],
        compiler_params=_params("parallel", "arbitrary"),
        name="ffn_gate",
    )(hn, hn, hn, w_in, w_in, taps)


def _ffn_down_body(u_ref, w_ref, x_ref, lng_ref, o_ref):
    k = pl.program_id(1)

    @pl.when(k == 0)
    def _():
        o_ref[...] = x_ref[...]

    d = o_ref.shape[1]
    for n0 in range(0, d, FFN_OUT_COLS):
        cols = slice(n0, min(n0 + FFN_OUT_COLS, d))
        o_ref[:, cols] += jnp.dot(u_ref[...], w_ref[:, cols], preferred_element_type=F32)

    @pl.when(k == pl.num_programs(1) - 1)
    def _():
        for r0 in range(0, o_ref.shape[0], SUBLANES):
            x = o_ref[r0:r0 + SUBLANES, :]
            var = jnp.mean(x * x, axis=-1, keepdims=True)
            o_ref[r0:r0 + SUBLANES, :] = x * lax.rsqrt(var + EPS) * lng_ref[...]


def _ffn_down(u, w_down, x1, ln_g, tm, tk):
    t, dff = u.shape
    d = w_down.shape[1]
    assert t % tm == 0 and dff % tk == 0
    return pl.pallas_call(
        _ffn_down_body,
        grid=(t // tm, dff // tk),
        in_specs=[pl.BlockSpec((tm, tk), lambda i, k: (i, k)),
                  pl.BlockSpec((tk, d), lambda i, k: (k, 0)),
                  pl.BlockSpec((tm, d), lambda i, k: (i, 0), pipeline_mode=pl.Buffered(1)),
                  pl.BlockSpec((1, d), lambda i, k: (0, 0))],
        out_specs=pl.BlockSpec((tm, d), lambda i, k: (i, 0)),
        out_shape=jax.ShapeDtypeStruct((t, d), F32),
        compiler_params=_params("parallel", "arbitrary"),
        name="ffn_down",
    )(u, w_down, x1, ln_g.reshape(1, d).astype(F32))


def _tile(n, pref):
    return pref if n % pref == 0 else n


def _trunk(x, w, bias_tbl):
    bsz, seq, d = x.shape
    t = bsz * seq
    x = x.reshape(t, d)
    da = N_HEADS * HEAD_DIM
    dl = LRU_BLOCKS * LRU_BLOCK
    tm = _tile(t, 1024)

    xn = _rmsnorm(x, w["ln_mix_g"], BF16, _tile(t, 512))
    qkv = _matmul(xn, w["w_in"], 0, 3 * da, BF16, tm, 1024, name="in_proj_qkv")
    xlyl = _matmul(xn, w["w_in"], 3 * da, 2 * dl, F32, tm, 1024, name="in_proj_lru")
    gates = _matmul(xn, w["w_in"], 3 * da + 2 * dl, 2 * d, F32, tm, 1024, name="in_proj_gates")

    attn = _attention(qkv, bias_tbl, bsz, seq)

    lru = _lru(xlyl, w["lru_conv_w"], w["lru_conv_b"], w["lru_wa"], w["lru_ba"], w["lru_wx"], w["lru_bx"],
               w["lru_lambda"], bsz, seq, _tile(seq, 512))

    merged = _merge(attn, lru, w["w_o_attn"], w["w_o_lru"], gates, w["b_gate"], tm, 512)
    x1 = _matmul(merged, w["w_out"], 0, d, F32, tm, 512, residual=x, name="out_proj")

    hn = _rmsnorm(x1, w["ln_ffn_g"], BF16, _tile(t, 512))
    u = _ffn_gate(hn, w["w_ffn_in"], w["ffn_conv_w"], w["ffn_conv_b"], seq, _tile(seq, 1024), 512)
    y = _ffn_down(u, w["w_ffn_out"], x1, w["ln_final_g"], _tile(t, 512), 1536)
    return y.reshape(bsz, seq, d)


def kernel(x_prompt, x_sample, ln_mix_g, w_in, b_gate, rpb, lru_conv_w, lru_conv_b, lru_wa, lru_ba, lru_wx,
           lru_bx, lru_lambda, w_o_attn, w_o_lru, w_out, ln_ffn_g, w_ffn_in, ffn_conv_w, ffn_conv_b, w_ffn_out,
           ln_final_g):
    assert ln_mix_g.shape[0] == 1, "single-layer block"
    w = dict(
        ln_mix_g=ln_mix_g[0].astype(F32), w_in=w_in[0].astype(BF16), b_gate=b_gate[0],
        lru_conv_w=lru_conv_w[0], lru_conv_b=lru_conv_b[0], lru_wa=lru_wa[0], lru_ba=lru_ba[0],
        lru_wx=lru_wx[0], lru_bx=lru_bx[0], lru_lambda=lru_lambda[0],
        w_o_attn=w_o_attn[0].astype(BF16), w_o_lru=w_o_lru[0].astype(BF16), w_out=w_out[0].astype(BF16),
        ln_ffn_g=ln_ffn_g[0].astype(F32), w_ffn_in=w_ffn_in[0].astype(BF16),
        ffn_conv_w=ffn_conv_w[0], ffn_conv_b=ffn_conv_b[0], w_ffn_out=w_ffn_out[0].astype(BF16),
        ln_final_g=ln_final_g.astype(F32),
    )
    bias_tbl = _attn_bias_table(rpb[0])
    return (_trunk(x_prompt, w, bias_tbl), _trunk(x_sample, w, bias_tbl))
```

```python
import functools

import jax
import jax.numpy as jnp
import numpy as np
from jax import lax
from jax.experimental import pallas as pl
from jax.experimental.pallas import tpu as pltpu

EPS = 1e-6
GRID_W = 64
N_HEADS = 16
HEAD_DIM = 128
WIN_R = 8
WIN_C = 16
ATTN_GROUP = 16
LRU_BLOCKS = 16
LRU_BLOCK = 128
LRU_CONV = 4
LRU_C = 8.0
FFN_CONV = 3
FFN_OUT_COLS = 1024
MASK_VALUE = float("-inf")

SUBLANES = 8
BF16_ROWS = 16
EW_ROWS = 16
VMEM_LIMIT_BYTES = 56 * 1024 * 1024

F32 = jnp.float32
BF16 = jnp.bfloat16


def _params(*semantics):
    return pltpu.CompilerParams(dimension_semantics=semantics, vmem_limit_bytes=VMEM_LIMIT_BYTES)


def _gelu_tanh(x):
    c = np.float32(np.sqrt(2.0 / np.pi))
    h = 0.5 * x
    return h + h * jnp.tanh(x * (c + np.float32(c * 0.044715) * (x * x)))


def _gelu_tanh_of_double(h):
    c = np.float32(np.sqrt(2.0 / np.pi))
    return h + h * jnp.tanh(h * (np.float32(2.0 * c) + np.float32(8.0 * c * 0.044715) * (h * h)))


def _sigmoid(x):
    return 0.5 * jnp.tanh(0.5 * x) + 0.5


def _rmsnorm_body(x_ref, g_ref, o_ref):
    x = x_ref[...]
    var = jnp.mean(x * x, axis=-1, keepdims=True)
    o_ref[...] = (x * lax.rsqrt(var + EPS) * g_ref[...]).astype(o_ref.dtype)


def _rmsnorm(x, g, out_dtype, tm):
    t, d = x.shape
    return pl.pallas_call(
        _rmsnorm_body,
        grid=(t // tm,),
        in_specs=[pl.BlockSpec((tm, d), lambda i: (i, 0)), pl.BlockSpec((1, d), lambda i: (0, 0))],
        out_specs=pl.BlockSpec((tm, d), lambda i: (i, 0)),
        out_shape=jax.ShapeDtypeStruct((t, d), out_dtype),
        compiler_params=_params("parallel"),
        name="rmsnorm",
    )(x, g.reshape(1, d))


def _matmul_body(x_ref, w_ref, o_ref):
    o_ref[...] = jnp.dot(x_ref[...], w_ref[...], preferred_element_type=F32).astype(o_ref.dtype)


def _matmul_res_body(x_ref, w_ref, r_ref, o_ref):
    o_ref[...] = r_ref[...] + jnp.dot(x_ref[...], w_ref[...], preferred_element_type=F32)


def _matmul(x, w, col0, ncols, out_dtype, tm, tn, residual=None, name="matmul"):
    t, k = x.shape
    assert w.shape[0] == k and col0 % tn == 0 and ncols % tn == 0 and t % tm == 0
    jb = col0 // tn
    in_specs = [pl.BlockSpec((tm, k), lambda i, j: (i, 0)), pl.BlockSpec((k, tn), lambda i, j: (0, j + jb))]
    args = [x, w]
    body = _matmul_body
    if residual is not None:
        in_specs.append(pl.BlockSpec((tm, tn), lambda i, j: (i, j)))
        args.append(residual)
        body = _matmul_res_body
    return pl.pallas_call(
        body,
        grid=(t // tm, ncols // tn),
        in_specs=in_specs,
        out_specs=pl.BlockSpec((tm, tn), lambda i, j: (i, j)),
        out_shape=jax.ShapeDtypeStruct((t, ncols), out_dtype),
        compiler_params=_params("parallel", "parallel"),
        name=name,
    )(*args)


def _attn_bias_table(rpb):
    cols = np.arange(GRID_W)
    col_start = np.clip(cols - WIN_C // 2, 0, GRID_W - WIN_C)
    key = np.arange(GRID_W)[None, :]
    valid = (key >= col_start[:, None]) & (key < col_start[:, None] + WIN_C)
    dc = np.clip(key - cols[:, None] + (WIN_C - 1), 0, 2 * WIN_C - 2)
    g = rpb.astype(F32)[:, :, dc]
    g = jnp.where(valid[None, None], g, MASK_VALUE)
    tbl = jnp.stack([g[:, v:v + WIN_R] for v in range(WIN_R)], axis=1)
    tbl = jnp.transpose(tbl, (0, 1, 3, 2, 4))
    return tbl.reshape(N_HEADS, WIN_R, GRID_W, WIN_R * GRID_W)


def _attn_body(q_ref, k_ref, v_ref, bias_ref, o_ref, s_scr, p_scr, *, rows):
    scale = HEAD_DIM ** -0.5
    nkeys = WIN_R * GRID_W
    ngroups = rows // ATTN_GROUP

    def window(g, j):
        r = g * ATTN_GROUP + j
        rs = jnp.clip(r - WIN_R // 2, 0, rows - WIN_R)
        return pl.multiple_of(r * GRID_W, GRID_W), pl.multiple_of(rs * GRID_W, GRID_W), rs - r + (WIN_R - 1)

    def scores(g):
        for j in range(ATTN_GROUP):
            q0, k0, _ = window(g, j)
            s_scr[j] = lax.dot_general(q_ref[pl.ds(q0, GRID_W), :], k_ref[pl.ds(k0, nkeys), :],
                                       (((1,), (1,)), ((), ())), preferred_element_type=F32)

    def softmax(g):
        for j in range(ATTN_GROUP):
            _, _, dr0 = window(g, j)
            s = s_scr[j] * scale + bias_ref[0, dr0]
            p = jnp.exp(s - jnp.max(s, axis=-1, keepdims=True))
            p = p / jnp.sum(p, axis=-1, keepdims=True)
            p_scr[j] = p.astype(BF16)

    def values(g):
        for j in range(ATTN_GROUP):
            q0, k0, _ = window(g, j)
            o = jnp.dot(p_scr[j], v_ref[pl.ds(k0, nkeys), :], preferred_element_type=F32)
            o_ref[pl.ds(q0, GRID_W), :] = o.astype(o_ref.dtype)

    scores(0)
    softmax(0)
    scores(1)

    def steady(g, carry):
        values(g - 2)
        softmax(g - 1)
        scores(g)
        return carry

    lax.fori_loop(2, ngroups, steady, 0)
    values(ngroups - 2)
    softmax(ngroups - 1)
    values(ngroups - 1)


def _attention(qkv, bias_tbl, bsz, seq):
    rows = seq // GRID_W
    nkeys = WIN_R * GRID_W
    assert seq % GRID_W == 0 and rows % ATTN_GROUP == 0 and rows // ATTN_GROUP >= 2 and rows >= WIN_R
    blk = lambda part: pl.BlockSpec((seq, HEAD_DIM), lambda b, h: (b, part * N_HEADS + h))
    return pl.pallas_call(
        functools.partial(_attn_body, rows=rows),
        grid=(bsz, N_HEADS),
        in_specs=[blk(0), blk(1), blk(2),
                  pl.BlockSpec((1, WIN_R, GRID_W, nkeys), lambda b, h: (h, 0, 0, 0))],
        out_specs=pl.BlockSpec((seq, HEAD_DIM), lambda b, h: (b, h)),
        out_shape=jax.ShapeDtypeStruct((bsz * seq, N_HEADS * HEAD_DIM), BF16),
        scratch_shapes=[pltpu.VMEM((ATTN_GROUP, GRID_W, nkeys), F32), pltpu.VMEM((ATTN_GROUP, GRID_W, nkeys), BF16)],
        compiler_params=_params("parallel", "parallel"),
        name="natten",
    )(qkv, qkv, qkv, bias_tbl)


def _lru_gates(x_block, n, w_ref, bias_ref, softplus, a_scr, b_scr):
    lanes = slice(n * LRU_BLOCK, (n + 1) * LRU_BLOCK)
    u = jnp.dot(x_block.astype(BF16), w_ref[n], preferred_element_type=F32)
    r = _sigmoid(u[:, :LRU_BLOCK] + bias_ref[0:1, lanes])
    i = _sigmoid(u[:, LRU_BLOCK:] + bias_ref[1:2, lanes])
    log_a = -LRU_C * r * softplus[:, lanes]
    a = jnp.exp(log_a)
    z = -jnp.tanh(log_a) * (a * a + 1.0)
    mult = jnp.where(z > 0.0, z * lax.rsqrt(z), 0.0)
    a_scr[:, lanes] = a
    b_scr[:, lanes] = mult * i * x_block


def _lru_softplus(lam_ref):
    lam = -lam_ref[...]
    return jnp.maximum(lam, 0.0) + jnp.log1p(jnp.exp(-jnp.abs(lam)))


def _lru_scan(a_scr, b_scr, hs_ref, h_scr, tc, reverse):
    ngroups = tc // SUBLANES

    def group_step(g, h):
        base = pl.multiple_of((ngroups - 1 - g if reverse else g) * SUBLANES, SUBLANES)
        for j in range(SUBLANES):
            row = base + (SUBLANES - 1 - j if reverse else j)
            h = a_scr[pl.ds(row, 1), :] * h + b_scr[pl.ds(row, 1), :]
            hs_ref[pl.ds(row, 1), :] = h
        return h

    h_scr[...] = lax.fori_loop(0, ngroups, group_step, h_scr[...])


def _lru_fwd_body(xp_ref, xc_ref, xn_ref, cw_ref, cb_ref, w_ref, bias_ref, lam_ref, h_ref, conv_ref,
                  xpad, a_scr, b_scr, h_scr, *, nchunks, tc):
    c = pl.program_id(1)

    @pl.when(c == 0)
    def _():
        h_scr[...] = jnp.zeros_like(h_scr)

    xpad[0:SUBLANES, :] = jnp.where(c == 0, 0.0, xp_ref[...])
    xpad[SUBLANES:SUBLANES + tc, :] = xc_ref[...]
    xpad[SUBLANES + tc:2 * SUBLANES + tc, :] = jnp.where(c == nchunks - 1, 0.0, xn_ref[...])
    softplus = _lru_softplus(lam_ref)
    for n in range(LRU_BLOCKS):
        lanes = slice(n * LRU_BLOCK, (n + 1) * LRU_BLOCK)
        xc = cb_ref[:, lanes]
        for k in range(LRU_CONV):
            xc = xc + cw_ref[k:k + 1, lanes] * xpad[pl.ds(SUBLANES - LRU_CONV // 2 + k, tc), lanes]
        conv_ref[:, lanes] = xc
        _lru_gates(xc, n, w_ref, bias_ref, softplus, a_scr, b_scr)
    _lru_scan(a_scr, b_scr, h_ref, h_scr, tc, reverse=False)


def _lru_bwd_body(conv_ref, yl_ref, hf_ref, w_ref, bias_ref, lam_ref, o_ref, a_scr, b_scr, hs_scr, h_scr, *, tc):
    @pl.when(pl.program_id(1) == 0)
    def _():
        h_scr[...] = jnp.zeros_like(h_scr)

    softplus = _lru_softplus(lam_ref)
    for n in range(LRU_BLOCKS):
        _lru_gates(conv_ref[:, n * LRU_BLOCK:(n + 1) * LRU_BLOCK], n, w_ref, bias_ref, softplus, a_scr, b_scr)
    _lru_scan(a_scr, b_scr, hs_scr, h_scr, tc, reverse=True)
    o_ref[...] = ((hf_ref[...] + hs_scr[...]) * _gelu_tanh(yl_ref[...])).astype(o_ref.dtype)


def _lru(xlyl, conv_w, conv_b, wa, ba, wx, bx, lam, bsz, seq, tc):
    t = bsz * seq
    dl = LRU_BLOCKS * LRU_BLOCK
    nchunks = seq // tc
    assert seq % tc == 0 and tc % SUBLANES == 0
    hb = tc // SUBLANES
    const = lambda shape: pl.BlockSpec(shape, lambda b, c: (0,) * len(shape))
    gate_specs = [const((LRU_BLOCKS, LRU_BLOCK, 2 * LRU_BLOCK)), const((2, dl)), const((1, dl))]

    def gate_args(d):
        return [jnp.concatenate([wa[d], wx[d]], axis=-1).astype(BF16),
                jnp.stack([ba[d], bx[d]]).astype(F32), lam[d].reshape(1, dl).astype(F32)]

    fwd = lambda b, c: b * nchunks + c
    bwd = lambda b, c: b * nchunks + nchunks - 1 - c
    h_fwd, conv = pl.pallas_call(
        functools.partial(_lru_fwd_body, nchunks=nchunks, tc=tc),
        grid=(bsz, nchunks),
        in_specs=[pl.BlockSpec((SUBLANES, dl), lambda b, c: (jnp.maximum(fwd(b, c) * hb - 1, 0), 0)),
                  pl.BlockSpec((tc, dl), lambda b, c: (fwd(b, c), 0)),
                  pl.BlockSpec((SUBLANES, dl), lambda b, c: (jnp.minimum((fwd(b, c) + 1) * hb, t // SUBLANES - 1), 0)),
                  const((LRU_CONV, dl)), const((1, dl))] + gate_specs,
        out_specs=[pl.BlockSpec((tc, dl), lambda b, c: (fwd(b, c), 0))] * 2,
        out_shape=[jax.ShapeDtypeStruct((t, dl), F32)] * 2,
        scratch_shapes=[pltpu.VMEM((tc + 2 * SUBLANES, dl), F32), pltpu.VMEM((tc, dl), F32),
                        pltpu.VMEM((tc, dl), F32), pltpu.VMEM((1, dl), F32)],
        compiler_params=_params("parallel", "arbitrary"),
        name="rglru_fwd",
    )(xlyl, xlyl, xlyl, conv_w.astype(F32), conv_b.reshape(1, dl).astype(F32), *gate_args(0))
    return pl.pallas_call(
        functools.partial(_lru_bwd_body, tc=tc),
        grid=(bsz, nchunks),
        in_specs=[pl.BlockSpec((tc, dl), lambda b, c: (bwd(b, c), 0)),
                  pl.BlockSpec((tc, dl), lambda b, c: (bwd(b, c), 1)),
                  pl.BlockSpec((tc, dl), lambda b, c: (bwd(b, c), 0))] + gate_specs,
        out_specs=pl.BlockSpec((tc, dl), lambda b, c: (bwd(b, c), 0)),
        out_shape=jax.ShapeDtypeStruct((t, dl), BF16),
        scratch_shapes=[pltpu.VMEM((tc, dl), F32), pltpu.VMEM((tc, dl), F32), pltpu.VMEM((tc, dl), F32),
                        pltpu.VMEM((1, dl), F32)],
        compiler_params=_params("parallel", "arbitrary"),
        name="rglru_bwd",
    )(conv, xlyl, h_fwd, *gate_args(1))


def _merge_body(a_ref, r_ref, wa_ref, wr_ref, ga_ref, gr_ref, bg_ref, o_ref):
    pa = jnp.dot(a_ref[...], wa_ref[...], preferred_element_type=F32)
    pr = jnp.dot(r_ref[...], wr_ref[...], preferred_element_type=F32)
    o = _sigmoid(ga_ref[...] + bg_ref[0:1, :]) * pa + _sigmoid(gr_ref[...] + bg_ref[1:2, :]) * pr
    o_ref[...] = o.astype(o_ref.dtype)


def _merge(attn, lru, w_oa, w_or, gates, b_gate, tm, tn):
    t, da = attn.shape
    d = w_oa.shape[1]
    nj = d // tn
    return pl.pallas_call(
        _merge_body,
        grid=(t // tm, nj),
        in_specs=[pl.BlockSpec((tm, da), lambda i, j: (i, 0)),
                  pl.BlockSpec((tm, lru.shape[1]), lambda i, j: (i, 0)),
                  pl.BlockSpec((da, tn), lambda i, j: (0, j)),
                  pl.BlockSpec((lru.shape[1], tn), lambda i, j: (0, j)),
                  pl.BlockSpec((tm, tn), lambda i, j: (i, j)),
                  pl.BlockSpec((tm, tn), lambda i, j: (i, j + nj)),
                  pl.BlockSpec((2, tn), lambda i, j: (0, j))],
        out_specs=pl.BlockSpec((tm, tn), lambda i, j: (i, j)),
        out_shape=jax.ShapeDtypeStruct((t, d), BF16),
        compiler_params=_params("parallel", "parallel"),
        name="gated_merge",
    )(attn, lru, w_oa, w_or, gates, gates, b_gate.astype(F32))


def _ffn_gate_body(xp_ref, xc_ref, xn_ref, wg_ref, wv_ref, cw_ref, o_ref, xs, g_st, v_st,
                   *, tiles_per_seq, tm):
    i = pl.program_id(0)
    j = pl.program_id(1)
    nj = pl.num_programs(1) - 1
    si = i % tiles_per_seq
    halo = BF16_ROWS

    def matmuls(slot):
        g_st[slot] = jnp.dot(xs[...], wg_ref[...], preferred_element_type=F32)
        v_st[slot] = jnp.dot(xs[halo:halo + tm, :], wv_ref[...], preferred_element_type=F32)

    def elementwise(slot):
        for r0 in range(0, tm, EW_ROWS):
            conv = cw_ref[FFN_CONV]
            for k in range(FFN_CONV):
                conv = conv + cw_ref[k] * g_st[slot, pl.ds(halo - FFN_CONV // 2 + k + r0, EW_ROWS), :]
            rows = slice(r0, r0 + EW_ROWS)
            o_ref[rows, :] = (_gelu_tanh_of_double(conv) * v_st[slot, rows, :]).astype(o_ref.dtype)

    @pl.when(j == 0)
    def _():
        xs[0:halo, :] = jnp.where(si == 0, jnp.zeros_like(xp_ref), xp_ref[...])
        xs[halo:halo + tm, :] = xc_ref[...]
        xs[halo + tm:2 * halo + tm, :] = jnp.where(si == tiles_per_seq - 1, jnp.zeros_like(xn_ref), xn_ref[...])
        matmuls(0)

    for parity in range(2):
        @pl.when(jnp.logical_and(jnp.logical_and(j > 0, j < nj), j % 2 == parity))
        def _():
            elementwise(1 - parity)
            matmuls(parity)

        @pl.when(jnp.logical_and(j == nj, j % 2 == parity))
        def _():
            elementwise(1 - parity)


def _ffn_gate(hn, w_in, conv_w, conv_b, seq, tm, tn):
    t, d = hn.shape
    dff = w_in.shape[1] // 2
    nj = dff // tn
    hb = tm // BF16_ROWS
    assert seq % tm == 0 and dff % tn == 0 and tm % BF16_ROWS == 0
    taps = 0.5 * jnp.concatenate([conv_w.astype(F32), conv_b.reshape(1, dff).astype(F32)])
    taps = jnp.broadcast_to(taps[:, None, :], (FFN_CONV + 1, EW_ROWS, dff))
    chunk = lambda j: jnp.minimum(j, nj - 1)
    done = lambda j: jnp.maximum(j - 1, 0)
    return pl.pallas_call(
        functools.partial(_ffn_gate_body, tiles_per_seq=seq // tm, tm=tm),
        grid=(t // tm, nj + 1),
        in_specs=[pl.BlockSpec((BF16_ROWS, d), lambda i, j: (jnp.maximum(i * hb - 1, 0), 0)),
                  pl.BlockSpec((tm, d), lambda i, j: (i, 0), pipeline_mode=pl.Buffered(1)),
                  pl.BlockSpec((BF16_ROWS, d), lambda i, j: (jnp.minimum((i + 1) * hb, t // BF16_ROWS - 1), 0)),
                  pl.BlockSpec((d, tn), lambda i, j: (0, chunk(j))),
                  pl.BlockSpec((d, tn), lambda i, j: (0, nj + chunk(j))),
                  pl.BlockSpec((FFN_CONV + 1, EW_ROWS, tn), lambda i, j: (0, 0, done(j)))],
        out_specs=pl.BlockSpec((tm, tn), lambda i, j: (i, done(j))),
        out_shape=jax.ShapeDtypeStruct((t, dff), BF16),
        scratch_shapes=[pltpu.VMEM((tm + 2 * BF16_ROWS, d), BF16), pltpu.VMEM((2, tm + 2 * BF16_ROWS, tn), F32),
                        pltpu.VMEM((2, tm, tn), F32)],
        compiler_params=_params("parallel", "arbitrary"),
        name="ffn_gate",
    )(hn, hn, hn, w_in, w_in, taps)


def _ffn_down_body(u_ref, w_ref, x_ref, lng_ref, o_ref):
    k = pl.program_id(1)

    d = o_ref.shape[1]

    def accumulate(acc_ref):
        for n0 in range(0, d, FFN_OUT_COLS):
            cols = slice(n0, min(n0 + FFN_OUT_COLS, d))
            o_ref[:, cols] = acc_ref[:, cols] + jnp.dot(u_ref[...], w_ref[:, cols], preferred_element_type=F32)

    @pl.when(k == 0)
    def _():
        accumulate(x_ref)

    @pl.when(k > 0)
    def _():
        accumulate(o_ref)

    @pl.when(k == pl.num_programs(1) - 1)
    def _():
        for r0 in range(0, o_ref.shape[0], SUBLANES):
            x = o_ref[r0:r0 + SUBLANES, :]
            var = jnp.mean(x * x, axis=-1, keepdims=True)
            o_ref[r0:r0 + SUBLANES, :] = x * lax.rsqrt(var + EPS) * lng_ref[...]


def _ffn_down(u, w_down, x1, ln_g, tm, tk):
    t, dff = u.shape
    d = w_down.shape[1]
    assert t % tm == 0 and dff % tk == 0
    return pl.pallas_call(
        _ffn_down_body,
        grid=(t // tm, dff // tk),
        in_specs=[pl.BlockSpec((tm, tk), lambda i, k: (i, k)),
                  pl.BlockSpec((tk, d), lambda i, k: (k, 0)),
                  pl.BlockSpec((tm, d), lambda i, k: (i, 0), pipeline_mode=pl.Buffered(1)),
                  pl.BlockSpec((1, d), lambda i, k: (0, 0))],
        out_specs=pl.BlockSpec((tm, d), lambda i, k: (i, 0)),
        out_shape=jax.ShapeDtypeStruct((t, d), F32),
        compiler_params=_params("parallel", "arbitrary"),
        name="ffn_down",
    )(u, w_down, x1, ln_g.reshape(1, d).astype(F32))


def _tile(n, pref):
    return pref if n % pref == 0 else n


def _trunk(x, w, bias_tbl):
    bsz, seq, d = x.shape
    t = bsz * seq
    x = x.reshape(t, d)
    da = N_HEADS * HEAD_DIM
    dl = LRU_BLOCKS * LRU_BLOCK
    tm = _tile(t, 1024)

    xn = _rmsnorm(x, w["ln_mix_g"], BF16, _tile(t, 512))
    qkv = _matmul(xn, w["w_in"], 0, 3 * da, BF16, tm, 1024, name="in_proj_qkv")
    xlyl = _matmul(xn, w["w_in"], 3 * da, 2 * dl, F32, tm, 1024, name="in_proj_lru")
    gates = _matmul(xn, w["w_in"], 3 * da + 2 * dl, 2 * d, F32, tm, 1024, name="in_proj_gates")

    attn = _attention(qkv, bias_tbl, bsz, seq)

    lru = _lru(xlyl, w["lru_conv_w"], w["lru_conv_b"], w["lru_wa"], w["lru_ba"], w["lru_wx"], w["lru_bx"],
               w["lru_lambda"], bsz, seq, _tile(seq, 512))

    merged = _merge(attn, lru, w["w_o_attn"], w["w_o_lru"], gates, w["b_gate"], tm, 512)
    x1 = _matmul(merged, w["w_out"], 0, d, F32, tm, 512, residual=x, name="out_proj")

    hn = _rmsnorm(x1, w["ln_ffn_g"], BF16, _tile(t, 512))
    u = _ffn_gate(hn, w["w_ffn_in"], w["ffn_conv_w"], w["ffn_conv_b"], seq, _tile(seq, 1024), 512)
    y = _ffn_down(u, w["w_ffn_out"], x1, w["ln_final_g"], _tile(t, 512), 1536)
    return y.reshape(bsz, seq, d)


def kernel(x_prompt, x_sample, ln_mix_g, w_in, b_gate, rpb, lru_conv_w, lru_conv_b, lru_wa, lru_ba, lru_wx,
           lru_bx, lru_lambda, w_o_attn, w_o_lru, w_out, ln_ffn_g, w_ffn_in, ffn_conv_w, ffn_conv_b, w_ffn_out,
           ln_final_g):
    assert ln_mix_g.shape[0] == 1, "single-layer block"
    w = dict(
        ln_mix_g=ln_mix_g[0].astype(F32), w_in=w_in[0].astype(BF16), b_gate=b_gate[0],
        lru_conv_w=lru_conv_w[0], lru_conv_b=lru_conv_b[0], lru_wa=lru_wa[0], lru_ba=lru_ba[0],
        lru_wx=lru_wx[0], lru_bx=lru_bx[0], lru_lambda=lru_lambda[0],
        w_o_attn=w_o_attn[0].astype(BF16), w_o_lru=w_o_lru[0].astype(BF16), w_out=w_out[0].astype(BF16),
        ln_ffn_g=ln_ffn_g[0].astype(F32), w_ffn_in=w_ffn_in[0].astype(BF16),
        ffn_conv_w=ffn_conv_w[0], ffn_conv_b=ffn_conv_b[0], w_ffn_out=w_ffn_out[0].astype(BF16),
        ln_final_g=ln_final_g.astype(F32),
    )
    bias_tbl = _attn_bias_table(rpb[0])
    return (_trunk(x_prompt, w, bias_tbl), _trunk(x_sample, w, bias_tbl))
```

```python
import functools

import jax
import jax.numpy as jnp
import numpy as np
from jax import lax
from jax.experimental import pallas as pl
from jax.experimental.pallas import tpu as pltpu

EPS = 1e-6
GRID_W = 64
N_HEADS = 16
HEAD_DIM = 128
WIN_R = 8
WIN_C = 16
ATTN_GROUP = 16
LRU_BLOCKS = 16
LRU_BLOCK = 128
LRU_CONV = 4
LRU_C = 8.0
FFN_CONV = 3
FFN_OUT_COLS = 1024
MASK_VALUE = float("-inf")

SUBLANES = 8
BF16_ROWS = 16
EW_ROWS = 16
VMEM_LIMIT_BYTES = 56 * 1024 * 1024

F32 = jnp.float32
BF16 = jnp.bfloat16


def _params(*semantics):
    return pltpu.CompilerParams(dimension_semantics=semantics, vmem_limit_bytes=VMEM_LIMIT_BYTES)


def _gelu_tanh(x):
    c = np.float32(np.sqrt(2.0 / np.pi))
    h = 0.5 * x
    return h + h * jnp.tanh(x * (c + np.float32(c * 0.044715) * (x * x)))


def _gelu_tanh_of_double(h):
    c = np.float32(np.sqrt(2.0 / np.pi))
    return h + h * jnp.tanh(h * (np.float32(2.0 * c) + np.float32(8.0 * c * 0.044715) * (h * h)))


def _sigmoid(x):
    return 0.5 * jnp.tanh(0.5 * x) + 0.5


def _rmsnorm_body(x_ref, g_ref, o_ref):
    x = x_ref[...]
    var = jnp.mean(x * x, axis=-1, keepdims=True)
    o_ref[...] = (x * lax.rsqrt(var + EPS) * g_ref[...]).astype(o_ref.dtype)


def _rmsnorm(x, g, out_dtype, tm):
    t, d = x.shape
    return pl.pallas_call(
        _rmsnorm_body,
        grid=(t // tm,),
        in_specs=[pl.BlockSpec((tm, d), lambda i: (i, 0)), pl.BlockSpec((1, d), lambda i: (0, 0))],
        out_specs=pl.BlockSpec((tm, d), lambda i: (i, 0)),
        out_shape=jax.ShapeDtypeStruct((t, d), out_dtype),
        compiler_params=_params("parallel"),
        name="rmsnorm",
    )(x, g.reshape(1, d))


def _matmul_body(x_ref, w_ref, o_ref):
    o_ref[...] = jnp.dot(x_ref[...], w_ref[...], preferred_element_type=F32).astype(o_ref.dtype)


def _matmul_res_body(x_ref, w_ref, r_ref, o_ref):
    o_ref[...] = r_ref[...] + jnp.dot(x_ref[...], w_ref[...], preferred_element_type=F32)


def _matmul(x, w, col0, ncols, out_dtype, tm, tn, residual=None, name="matmul"):
    t, k = x.shape
    assert w.shape[0] == k and col0 % tn == 0 and ncols % tn == 0 and t % tm == 0
    jb = col0 // tn
    in_specs = [pl.BlockSpec((tm, k), lambda i, j: (i, 0)), pl.BlockSpec((k, tn), lambda i, j: (0, j + jb))]
    args = [x, w]
    body = _matmul_body
    if residual is not None:
        in_specs.append(pl.BlockSpec((tm, tn), lambda i, j: (i, j)))
        args.append(residual)
        body = _matmul_res_body
    return pl.pallas_call(
        body,
        grid=(t // tm, ncols // tn),
        in_specs=in_specs,
        out_specs=pl.BlockSpec((tm, tn), lambda i, j: (i, j)),
        out_shape=jax.ShapeDtypeStruct((t, ncols), out_dtype),
        compiler_params=_params("parallel", "parallel"),
        name=name,
    )(*args)


def _attn_bias_table(rpb):
    cols = np.arange(GRID_W)
    col_start = np.clip(cols - WIN_C // 2, 0, GRID_W - WIN_C)
    key = np.arange(GRID_W)[None, :]
    valid = (key >= col_start[:, None]) & (key < col_start[:, None] + WIN_C)
    dc = np.clip(key - cols[:, None] + (WIN_C - 1), 0, 2 * WIN_C - 2)
    g = rpb.astype(F32)[:, :, dc]
    g = jnp.where(valid[None, None], g, MASK_VALUE)
    tbl = jnp.stack([g[:, v:v + WIN_R] for v in range(WIN_R)], axis=1)
    tbl = jnp.transpose(tbl, (0, 1, 3, 2, 4))
    return tbl.reshape(N_HEADS, WIN_R, GRID_W, WIN_R * GRID_W)


def _attn_body(q_ref, k_ref, v_ref, bias_ref, o_ref, s_scr, p_scr, *, rows):
    scale = HEAD_DIM ** -0.5
    nkeys = WIN_R * GRID_W
    ngroups = rows // ATTN_GROUP

    def window(g, j):
        r = g * ATTN_GROUP + j
        rs = jnp.clip(r - WIN_R // 2, 0, rows - WIN_R)
        return pl.multiple_of(r * GRID_W, GRID_W), pl.multiple_of(rs * GRID_W, GRID_W), rs - r + (WIN_R - 1)

    def scores(g):
        for j in range(ATTN_GROUP):
            q0, k0, _ = window(g, j)
            s_scr[j] = lax.dot_general(q_ref[pl.ds(q0, GRID_W), :], k_ref[pl.ds(k0, nkeys), :],
                                       (((1,), (1,)), ((), ())), preferred_element_type=F32)

    def softmax(g):
        for j in range(ATTN_GROUP):
            _, _, dr0 = window(g, j)
            s = s_scr[j] * scale + bias_ref[0, dr0]
            p = jnp.exp(s - jnp.max(s, axis=-1, keepdims=True))
            p = p / jnp.sum(p, axis=-1, keepdims=True)
            p_scr[j] = p.astype(BF16)

    def values(g):
        for j in range(ATTN_GROUP):
            q0, k0, _ = window(g, j)
            o = jnp.dot(p_scr[j], v_ref[pl.ds(k0, nkeys), :], preferred_element_type=F32)
            o_ref[pl.ds(q0, GRID_W), :] = o.astype(o_ref.dtype)

    scores(0)
    softmax(0)
    scores(1)

    def steady(g, carry):
        values(g - 2)
        softmax(g - 1)
        scores(g)
        return carry

    lax.fori_loop(2, ngroups, steady, 0)
    values(ngroups - 2)
    softmax(ngroups - 1)
    values(ngroups - 1)


def _attention(qkv, bias_tbl, bsz, seq):
    rows = seq // GRID_W
    nkeys = WIN_R * GRID_W
    assert seq % GRID_W == 0 and rows % ATTN_GROUP == 0 and rows // ATTN_GROUP >= 2 and rows >= WIN_R
    blk = lambda part: pl.BlockSpec((seq, HEAD_DIM), lambda b, h: (b, part * N_HEADS + h))
    return pl.pallas_call(
        functools.partial(_attn_body, rows=rows),
        grid=(bsz, N_HEADS),
        in_specs=[blk(0), blk(1), blk(2),
                  pl.BlockSpec((1, WIN_R, GRID_W, nkeys), lambda b, h: (h, 0, 0, 0))],
        out_specs=pl.BlockSpec((seq, HEAD_DIM), lambda b, h: (b, h)),
        out_shape=jax.ShapeDtypeStruct((bsz * seq, N_HEADS * HEAD_DIM), BF16),
        scratch_shapes=[pltpu.VMEM((ATTN_GROUP, GRID_W, nkeys), F32), pltpu.VMEM((ATTN_GROUP, GRID_W, nkeys), BF16)],
        compiler_params=_params("parallel", "parallel"),
        name="natten",
    )(qkv, qkv, qkv, bias_tbl)


def _lru_gates(x_block, n, w_ref, bias_ref, softplus, a_scr, b_scr):
    lanes = slice(n * LRU_BLOCK, (n + 1) * LRU_BLOCK)
    u = jnp.dot(x_block.astype(BF16), w_ref[n], preferred_element_type=F32)
    r = _sigmoid(u[:, :LRU_BLOCK] + bias_ref[0:1, lanes])
    i = _sigmoid(u[:, LRU_BLOCK:] + bias_ref[1:2, lanes])
    log_a = -LRU_C * r * softplus[:, lanes]
    a = jnp.exp(log_a)
    z = -jnp.tanh(log_a) * (a * a + 1.0)
    mult = jnp.where(z > 0.0, z * lax.rsqrt(z), 0.0)
    a_scr[:, lanes] = a
    b_scr[:, lanes] = mult * i * x_block


def _lru_softplus(lam_ref):
    lam = -lam_ref[...]
    return jnp.maximum(lam, 0.0) + jnp.log1p(jnp.exp(-jnp.abs(lam)))


def _lru_scan(a_scr, b_scr, hs_ref, h_scr, tc, reverse):
    ngroups = tc // SUBLANES

    def group_step(g, h):
        base = pl.multiple_of((ngroups - 1 - g if reverse else g) * SUBLANES, SUBLANES)
        for j in range(SUBLANES):
            row = base + (SUBLANES - 1 - j if reverse else j)
            h = a_scr[pl.ds(row, 1), :] * h + b_scr[pl.ds(row, 1), :]
            hs_ref[pl.ds(row, 1), :] = h
        return h

    h_scr[...] = lax.fori_loop(0, ngroups, group_step, h_scr[...])


def _lru_fwd_body(xp_ref, xc_ref, xn_ref, cw_ref, cb_ref, w_ref, bias_ref, lam_ref, h_ref, conv_ref,
                  xpad, a_scr, b_scr, h_scr, *, nchunks, tc):
    c = pl.program_id(1)

    @pl.when(c == 0)
    def _():
        h_scr[...] = jnp.zeros_like(h_scr)

    xpad[0:SUBLANES, :] = jnp.where(c == 0, 0.0, xp_ref[...])
    xpad[SUBLANES:SUBLANES + tc, :] = xc_ref[...]
    xpad[SUBLANES + tc:2 * SUBLANES + tc, :] = jnp.where(c == nchunks - 1, 0.0, xn_ref[...])
    softplus = _lru_softplus(lam_ref)
    for n in range(LRU_BLOCKS):
        lanes = slice(n * LRU_BLOCK, (n + 1) * LRU_BLOCK)
        xc = cb_ref[:, lanes]
        for k in range(LRU_CONV):
            xc = xc + cw_ref[k:k + 1, lanes] * xpad[pl.ds(SUBLANES - LRU_CONV // 2 + k, tc), lanes]
        conv_ref[:, lanes] = xc
        _lru_gates(xc, n, w_ref, bias_ref, softplus, a_scr, b_scr)
    _lru_scan(a_scr, b_scr, h_ref, h_scr, tc, reverse=False)


def _lru_bwd_body(conv_ref, yl_ref, hf_ref, w_ref, bias_ref, lam_ref, o_ref, a_scr, b_scr, hs_scr, h_scr, *, tc):
    @pl.when(pl.program_id(1) == 0)
    def _():
        h_scr[...] = jnp.zeros_like(h_scr)

    softplus = _lru_softplus(lam_ref)
    for n in range(LRU_BLOCKS):
        _lru_gates(conv_ref[:, n * LRU_BLOCK:(n + 1) * LRU_BLOCK], n, w_ref, bias_ref, softplus, a_scr, b_scr)
    _lru_scan(a_scr, b_scr, hs_scr, h_scr, tc, reverse=True)
    o_ref[...] = ((hf_ref[...] + hs_scr[...]) * _gelu_tanh(yl_ref[...])).astype(o_ref.dtype)


def _lru(xlyl, conv_w, conv_b, wa, ba, wx, bx, lam, bsz, seq, tc):
    t = bsz * seq
    dl = LRU_BLOCKS * LRU_BLOCK
    nchunks = seq // tc
    assert seq % tc == 0 and tc % SUBLANES == 0
    hb = tc // SUBLANES
    const = lambda shape: pl.BlockSpec(shape, lambda b, c: (0,) * len(shape))
    gate_specs = [const((LRU_BLOCKS, LRU_BLOCK, 2 * LRU_BLOCK)), const((2, dl)), const((1, dl))]

    def gate_args(d):
        return [jnp.concatenate([wa[d], wx[d]], axis=-1).astype(BF16),
                jnp.stack([ba[d], bx[d]]).astype(F32), lam[d].reshape(1, dl).astype(F32)]

    fwd = lambda b, c: b * nchunks + c
    bwd = lambda b, c: b * nchunks + nchunks - 1 - c
    h_fwd, conv = pl.pallas_call(
        functools.partial(_lru_fwd_body, nchunks=nchunks, tc=tc),
        grid=(bsz, nchunks),
        in_specs=[pl.BlockSpec((SUBLANES, dl), lambda b, c: (jnp.maximum(fwd(b, c) * hb - 1, 0), 0)),
                  pl.BlockSpec((tc, dl), lambda b, c: (fwd(b, c), 0)),
                  pl.BlockSpec((SUBLANES, dl), lambda b, c: (jnp.minimum((fwd(b, c) + 1) * hb, t // SUBLANES - 1), 0)),
                  const((LRU_CONV, dl)), const((1, dl))] + gate_specs,
        out_specs=[pl.BlockSpec((tc, dl), lambda b, c: (fwd(b, c), 0))] * 2,
        out_shape=[jax.ShapeDtypeStruct((t, dl), F32)] * 2,
        scratch_shapes=[pltpu.VMEM((tc + 2 * SUBLANES, dl), F32), pltpu.VMEM((tc, dl), F32),
                        pltpu.VMEM((tc, dl), F32), pltpu.VMEM((1, dl), F32)],
        compiler_params=_params("parallel", "arbitrary"),
        name="rglru_fwd",
    )(xlyl, xlyl, xlyl, conv_w.astype(F32), conv_b.reshape(1, dl).astype(F32), *gate_args(0))
    return pl.pallas_call(
        functools.partial(_lru_bwd_body, tc=tc),
        grid=(bsz, nchunks),
        in_specs=[pl.BlockSpec((tc, dl), lambda b, c: (bwd(b, c), 0)),
                  pl.BlockSpec((tc, dl), lambda b, c: (bwd(b, c), 1)),
                  pl.BlockSpec((tc, dl), lambda b, c: (bwd(b, c), 0))] + gate_specs,
        out_specs=pl.BlockSpec((tc, dl), lambda b, c: (bwd(b, c), 0)),
        out_shape=jax.ShapeDtypeStruct((t, dl), BF16),
        scratch_shapes=[pltpu.VMEM((tc, dl), F32), pltpu.VMEM((tc, dl), F32), pltpu.VMEM((tc, dl), F32),
                        pltpu.VMEM((1, dl), F32)],
        compiler_params=_params("parallel", "arbitrary"),
        name="rglru_bwd",
    )(conv, xlyl, h_fwd, *gate_args(1))


def _merge_body(a_ref, r_ref, wa_ref, wr_ref, ga_ref, gr_ref, bg_ref, o_ref):
    pa = jnp.dot(a_ref[...], wa_ref[...], preferred_element_type=F32)
    pr = jnp.dot(r_ref[...], wr_ref[...], preferred_element_type=F32)
    o = _sigmoid(ga_ref[...] + bg_ref[0:1, :]) * pa + _sigmoid(gr_ref[...] + bg_ref[1:2, :]) * pr
    o_ref[...] = o.astype(o_ref.dtype)


def _merge(attn, lru, w_oa, w_or, gates, b_gate, tm, tn):
    t, da = attn.shape
    d = w_oa.shape[1]
    nj = d // tn
    return pl.pallas_call(
        _merge_body,
        grid=(t // tm, nj),
        in_specs=[pl.BlockSpec((tm, da), lambda i, j: (i, 0)),
                  pl.BlockSpec((tm, lru.shape[1]), lambda i, j: (i, 0)),
                  pl.BlockSpec((da, tn), lambda i, j: (0, j)),
                  pl.BlockSpec((lru.shape[1], tn), lambda i, j: (0, j)),
                  pl.BlockSpec((tm, tn), lambda i, j: (i, j)),
                  pl.BlockSpec((tm, tn), lambda i, j: (i, j + nj)),
                  pl.BlockSpec((2, tn), lambda i, j: (0, j))],
        out_specs=pl.BlockSpec((tm, tn), lambda i, j: (i, j)),
        out_shape=jax.ShapeDtypeStruct((t, d), BF16),
        compiler_params=_params("parallel", "parallel"),
        name="gated_merge",
    )(attn, lru, w_oa, w_or, gates, gates, b_gate.astype(F32))


def _ffn_gate_body(xp_ref, xc_ref, xn_ref, wg_ref, wv_ref, cw_ref, o_ref, xs, g_st, v_st,
                   *, tiles_per_seq, tm):
    i = pl.program_id(0)
    j = pl.program_id(1)
    nj = pl.num_programs(1) - 1
    si = i % tiles_per_seq
    halo = BF16_ROWS

    def matmuls(slot):
        g_st[slot] = jnp.dot(xs[...], wg_ref[...], preferred_element_type=F32)
        v_st[slot] = jnp.dot(xs[halo:halo + tm, :], wv_ref[...], preferred_element_type=F32)

    def elementwise(slot):
        for r0 in range(0, tm, EW_ROWS):
            conv = cw_ref[FFN_CONV]
            for k in range(FFN_CONV):
                conv = conv + cw_ref[k] * g_st[slot, pl.ds(halo - FFN_CONV // 2 + k + r0, EW_ROWS), :]
            rows = slice(r0, r0 + EW_ROWS)
            o_ref[rows, :] = (_gelu_tanh_of_double(conv) * v_st[slot, rows, :]).astype(o_ref.dtype)

    @pl.when(j == 0)
    def _():
        xs[0:halo, :] = jnp.where(si == 0, jnp.zeros_like(xp_ref), xp_ref[...])
        xs[halo:halo + tm, :] = xc_ref[...]
        xs[halo + tm:2 * halo + tm, :] = jnp.where(si == tiles_per_seq - 1, jnp.zeros_like(xn_ref), xn_ref[...])
        matmuls(0)

    for parity in range(2):
        @pl.when(jnp.logical_and(jnp.logical_and(j > 0, j < nj), j % 2 == parity))
        def _():
            elementwise(1 - parity)
            matmuls(parity)

        @pl.when(jnp.logical_and(j == nj, j % 2 == parity))
        def _():
            elementwise(1 - parity)


def _ffn_gate(hn, w_in, conv_w, conv_b, seq, tm, tn):
    t, d = hn.shape
    dff = w_in.shape[1] // 2
    nj = dff // tn
    hb = tm // BF16_ROWS
    assert seq % tm == 0 and dff % tn == 0 and tm % BF16_ROWS == 0
    taps = 0.5 * jnp.concatenate([conv_w.astype(F32), conv_b.reshape(1, dff).astype(F32)])
    taps = jnp.broadcast_to(taps[:, None, :], (FFN_CONV + 1, EW_ROWS, dff))
    chunk = lambda j: jnp.minimum(j, nj - 1)
    done = lambda j: jnp.maximum(j - 1, 0)
    return pl.pallas_call(
        functools.partial(_ffn_gate_body, tiles_per_seq=seq // tm, tm=tm),
        grid=(t // tm, nj + 1),
        in_specs=[pl.BlockSpec((BF16_ROWS, d), lambda i, j: (jnp.maximum(i * hb - 1, 0), 0)),
                  pl.BlockSpec((tm, d), lambda i, j: (i, 0), pipeline_mode=pl.Buffered(1)),
                  pl.BlockSpec((BF16_ROWS, d), lambda i, j: (jnp.minimum((i + 1) * hb, t // BF16_ROWS - 1), 0)),
                  pl.BlockSpec((d, tn), lambda i, j: (0, chunk(j))),
                  pl.BlockSpec((d, tn), lambda i, j: (0, nj + chunk(j))),
                  pl.BlockSpec((FFN_CONV + 1, EW_ROWS, tn), lambda i, j: (0, 0, done(j)))],
        out_specs=pl.BlockSpec((tm, tn), lambda i, j: (i, done(j))),
        out_shape=jax.ShapeDtypeStruct((t, dff), BF16),
        scratch_shapes=[pltpu.VMEM((tm + 2 * BF16_ROWS, d), BF16), pltpu.VMEM((2, tm + 2 * BF16_ROWS, tn), F32),
                        pltpu.VMEM((2, tm, tn), F32)],
        compiler_params=_params("parallel", "arbitrary"),
        name="ffn_gate",
    )(hn, hn, hn, w_in, w_in, taps)


def _ffn_down_body(u_ref, w_ref, x_ref, lng_ref, o_ref):
    k = pl.program_id(1)

    d = o_ref.shape[1]

    def accumulate(acc_ref):
        for n0 in range(0, d, FFN_OUT_COLS):
            cols = slice(n0, min(n0 + FFN_OUT_COLS, d))
            o_ref[:, cols] = acc_ref[:, cols] + jnp.dot(u_ref[...], w_ref[:, cols], preferred_element_type=F32)

    @pl.when(k == 0)
    def _():
        accumulate(x_ref)

    @pl.when(k > 0)
    def _():
        accumulate(o_ref)

    @pl.when(k == pl.num_programs(1) - 1)
    def _():
        for r0 in range(0, o_ref.shape[0], SUBLANES):
            x = o_ref[r0:r0 + SUBLANES, :]
            var = jnp.mean(x * x, axis=-1, keepdims=True)
            o_ref[r0:r0 + SUBLANES, :] = x * lax.rsqrt(var + EPS) * lng_ref[...]


def _ffn_down(u, w_down, x1, ln_g, tm, tk):
    t, dff = u.shape
    d = w_down.shape[1]
    assert t % tm == 0 and dff % tk == 0
    return pl.pallas_call(
        _ffn_down_body,
        grid=(t // tm, dff // tk),
        in_specs=[pl.BlockSpec((tm, tk), lambda i, k: (i, k)),
                  pl.BlockSpec((tk, d), lambda i, k: (k, 0)),
                  pl.BlockSpec((tm, d), lambda i, k: (i, 0), pipeline_mode=pl.Buffered(1)),
                  pl.BlockSpec((1, d), lambda i, k: (0, 0))],
        out_specs=pl.BlockSpec((tm, d), lambda i, k: (i, 0)),
        out_shape=jax.ShapeDtypeStruct((t, d), F32),
        compiler_params=_params("parallel", "arbitrary"),
        name="ffn_down",
    )(u, w_down, x1, ln_g.reshape(1, d).astype(F32))


def _tile(n, pref):
    return pref if n % pref == 0 else n


def _trunk(x, w, bias_tbl):
    bsz, seq, d = x.shape
    t = bsz * seq
    x = x.reshape(t, d)
    da = N_HEADS * HEAD_DIM
    dl = LRU_BLOCKS * LRU_BLOCK
    tm = _tile(t, 1024)

    xn = _rmsnorm(x, w["ln_mix_g"], BF16, _tile(t, 512))
    qkv = _matmul(xn, w["w_in"], 0, 3 * da, BF16, tm, 1024, name="in_proj_qkv")
    xlyl = _matmul(xn, w["w_in"], 3 * da, 2 * dl, F32, tm, 1024, name="in_proj_lru")
    gates = _matmul(xn, w["w_in"], 3 * da + 2 * dl, 2 * d, F32, tm, 1024, name="in_proj_gates")

    attn = _attention(qkv, bias_tbl, bsz, seq)

    lru = _lru(xlyl, w["lru_conv_w"], w["lru_conv_b"], w["lru_wa"], w["lru_ba"], w["lru_wx"], w["lru_bx"],
               w["lru_lambda"], bsz, seq, _tile(seq, 512))

    merged = _merge(attn, lru, w["w_o_attn"], w["w_o_lru"], gates, w["b_gate"], tm, 512)
    x1 = _matmul(merged, w["w_out"], 0, d, F32, tm, 1024, residual=x, name="out_proj")

    hn = _rmsnorm(x1, w["ln_ffn_g"], BF16, _tile(t, 512))
    u = _ffn_gate(hn, w["w_ffn_in"], w["ffn_conv_w"], w["ffn_conv_b"], seq, _tile(seq, 1024), 512)
    y = _ffn_down(u, w["w_ffn_out"], x1, w["ln_final_g"], _tile(t, 512), 1536)
    return y.reshape(bsz, seq, d)


def kernel(x_prompt, x_sample, ln_mix_g, w_in, b_gate, rpb, lru_conv_w, lru_conv_b, lru_wa, lru_ba, lru_wx,
           lru_bx, lru_lambda, w_o_attn, w_o_lru, w_out, ln_ffn_g, w_ffn_in, ffn_conv_w, ffn_conv_b, w_ffn_out,
           ln_final_g):
    assert ln_mix_g.shape[0] == 1, "single-layer block"
    w = dict(
        ln_mix_g=ln_mix_g[0].astype(F32), w_in=w_in[0].astype(BF16), b_gate=b_gate[0],
        lru_conv_w=lru_conv_w[0], lru_conv_b=lru_conv_b[0], lru_wa=lru_wa[0], lru_ba=lru_ba[0],
        lru_wx=lru_wx[0], lru_bx=lru_bx[0], lru_lambda=lru_lambda[0],
        w_o_attn=w_o_attn[0].astype(BF16), w_o_lru=w_o_lru[0].astype(BF16), w_out=w_out[0].astype(BF16),
        ln_ffn_g=ln_ffn_g[0].astype(F32), w_ffn_in=w_ffn_in[0].astype(BF16),
        ffn_conv_w=ffn_conv_w[0], ffn_conv_b=ffn_conv_b[0], w_ffn_out=w_ffn_out[0].astype(BF16),
        ln_final_g=ln_final_g.astype(F32),
    )
    bias_tbl = _attn_bias_table(rpb[0])
    return (_trunk(x_prompt, w, bias_tbl), _trunk(x_sample, w, bias_tbl))
```
